```python
import math
import jax, jax.numpy as jnp
from jax import lax
import numpy as np

D_MODEL = 2048
BATCH = 2
SEQ = 4096
DEPTH = 1
DEC_BATCH = 8
DEC_SEQ = 16
PAST_LEN = 2048

CHUNK = 64
EPS = 1e-6
D_FF = 4 * D_MODEL
SSD_EXPAND = 2
D_INNER = SSD_EXPAND * D_MODEL
SSD_HEADDIM = 64
SSD_HEADS = D_INNER // SSD_HEADDIM
SSD_GROUPS = 8
SSD_STATE = 128
SSD_CONV = 4
XBC_DIM = D_INNER + 2 * SSD_GROUPS * SSD_STATE
D_CONV = D_MODEL
CM_WIDTH = 31
N_MEM = 256
XA_HEADS = 4
XA_HEAD_DIM = D_MODEL // XA_HEADS
N_BRANCH = 2
IN_COLS = D_INNER + XBC_DIM + SSD_HEADS + 2 * D_CONV + N_BRANCH * D_MODEL

kernel_name = "hybrid_ssd_conformer_stream_step"


def rmsnorm(x, g):
    xf = x.astype(jnp.float32)
    y = xf * lax.rsqrt(jnp.mean(xf * xf, axis=-1, keepdims=True) + EPS)
    return (y * g.astype(jnp.float32)).astype(x.dtype)


def group_rmsnorm(x, g, groups):
    shp = x.shape
    xf = x.astype(jnp.float32).reshape(shp[:-1] + (groups, shp[-1] // groups))
    y = xf * lax.rsqrt(jnp.mean(xf * xf, axis=-1, keepdims=True) + EPS)
    return (y.reshape(shp) * g.astype(jnp.float32)).astype(x.dtype)


def layernorm(x, g, b):
    xf = x.astype(jnp.float32)
    mu = jnp.mean(xf, axis=-1, keepdims=True)
    var = jnp.mean(jnp.square(xf - mu), axis=-1, keepdims=True)
    y = (xf - mu) * lax.rsqrt(var + EPS)
    return (y * g.astype(jnp.float32) + b.astype(jnp.float32)).astype(x.dtype)


def swiglu_ffn(h, wi, wo):
    a, b = jnp.split(h @ wi, 2, axis=-1)
    return (jax.nn.silu(a) * b) @ wo


def causal_dwconv(u, prev, w, bias):
    k = w.shape[0]
    full = jnp.concatenate([prev.astype(u.dtype), u], axis=1)
    y = lax.conv_general_dilated(full, w.astype(u.dtype)[:, None, :], window_strides=(1,), padding="VALID",
                                 dimension_numbers=("NWC", "WIO", "NWC"), feature_group_count=u.shape[-1])
    return y + bias, full[:, full.shape[1] - (k - 1):]


def ssd_scan(x, dA, Bm, Cm, h0):
    b, seq, nh, hp = x.shape
    ng, ns = Bm.shape[2], Bm.shape[3]
    nr = nh // ng
    t = min(CHUNK, seq)
    nc = seq // t
    f32 = jnp.float32
    xc = x.astype(f32).reshape(b, nc, t, ng, nr, hp)
    Bc = Bm.astype(f32).reshape(b, nc, t, ng, ns)
    Cc = Cm.astype(f32).reshape(b, nc, t, ng, ns)
    A = jnp.transpose(dA.astype(f32).reshape(b, nc, t, ng, nr), (0, 3, 4, 1, 2))
    a_cs = jnp.cumsum(A, axis=-1)
    mask = jnp.tril(jnp.ones((t, t), dtype=bool))
    seg = a_cs[..., :, None] - a_cs[..., None, :]
    lmat = jnp.exp(jnp.where(mask, seg, -jnp.inf))
    y_diag = jnp.einsum("bclgn,bcsgn,bgrcls,bcsgrp->bclgrp", Cc, Bc, lmat, xc)
    decay_states = jnp.exp(a_cs[..., -1:] - a_cs)
    states = jnp.einsum("bcsgn,bgrcs,bcsgrp->cbgrpn", Bc, decay_states, xc)
    chunk_decay = jnp.moveaxis(jnp.exp(a_cs[..., -1]), -1, 0)

    def step(h, inp):
        dec, st = inp
        return h * dec[..., None, None] + st, h

    h_init = h0.astype(f32).reshape(b, ng, nr, hp, ns)
    h_fin, h_prev = lax.scan(step, h_init, (chunk_decay, states))
    y_off = jnp.einsum("bclgn,cbgrpn,bgrcl->bclgrp", Cc, h_prev, jnp.exp(a_cs))
    y = (y_diag + y_off).reshape(b, seq, nh, hp).astype(x.dtype)
    return y, h_fin.reshape(b, nh, hp, ns).astype(h0.dtype)


def parallel_mixer(h, ssd_prev, ssm_h0, cm_prev, w_in, ssd_conv_w, ssd_conv_b, ssd_dt_bias, ssd_a_log,
                   ssd_d, ssd_norm, ssd_w_out, cm_dw_w, cm_dw_b, cm_ln_g, cm_ln_b, cm_w_out, w_mix_out):
    b, seq = h.shape[0], h.shape[1]
    proj = h @ w_in
    i1 = D_INNER
    i2 = i1 + XBC_DIM
    i3 = i2 + SSD_HEADS
    i4 = i3 + D_CONV
    i5 = i4 + D_CONV
    z, xbc, dt, cm_val, cm_gate, gates = jnp.split(proj, [i1, i2, i3, i4, i5], axis=-1)
    xbc, new_ssd_conv = causal_dwconv(xbc, ssd_prev, ssd_conv_w, ssd_conv_b)
    xbc = jax.nn.silu(xbc)
    xs, Bm, Cm = jnp.split(xbc, [D_INNER, D_INNER + SSD_GROUPS * SSD_STATE], axis=-1)
    xs = xs.reshape(b, seq, SSD_HEADS, SSD_HEADDIM)
    Bm = Bm.reshape(b, seq, SSD_GROUPS, SSD_STATE)
    Cm = Cm.reshape(b, seq, SSD_GROUPS, SSD_STATE)
    dtp = jax.nn.softplus(dt.astype(jnp.float32) + ssd_dt_bias.astype(jnp.float32))
    a = -jnp.exp(ssd_a_log.astype(jnp.float32))
    y, new_h = ssd_scan(xs * dtp[..., None].astype(xs.dtype), dtp * a, Bm, Cm, ssm_h0)
    y = (y + ssd_d[:, None] * xs).reshape(b, seq, D_INNER)
    y = group_rmsnorm(y * jax.nn.silu(z), ssd_norm, SSD_GROUPS)
    ssd_out = y @ ssd_w_out
    u = cm_val * jax.nn.sigmoid(cm_gate)
    u, new_cm_conv = causal_dwconv(u, cm_prev, cm_dw_w, cm_dw_b)
    u = jax.nn.silu(layernorm(u, cm_ln_g, cm_ln_b))
    cm_out = u @ cm_w_out
    g_ssd, g_cm = jnp.split(gates, 2, axis=-1)
    m = jax.nn.sigmoid(g_ssd) * ssd_out + jax.nn.sigmoid(g_cm) * cm_out
    return m @ w_mix_out, new_ssd_conv, new_h, new_cm_conv


def mem_kv(mem, g, w):
    m = rmsnorm(mem, g)
    return (m @ w).reshape(mem.shape[0], mem.shape[1], XA_HEADS, XA_HEAD_DIM)


def mem_attend(h, k, v, wq, wo):
    b, seq = h.shape[0], h.shape[1]
    q = (h @ wq).reshape(b, seq, XA_HEADS, XA_HEAD_DIM)
    s = jnp.einsum("blhd,bmhd->bhlm", q, k.astype(q.dtype)).astype(jnp.float32) * (XA_HEAD_DIM ** -0.5)
    pr = jax.nn.softmax(s, axis=-1).astype(h.dtype)
    o = jnp.einsum("bhlm,bmhd->blhd", pr, v.astype(h.dtype)).reshape(b, seq, D_MODEL)
    return o @ wo


def run_trunk(x, ssd_prev, ssm_prev, cm_prev, mem_k, mem_v, ffn1_norm, ffn1_wi, ffn1_wo, mix_norm, w_in,
              ssd_conv_w, ssd_conv_b, ssd_dt_bias, ssd_a_log, ssd_d, ssd_norm, ssd_w_out, cm_dw_w, cm_dw_b,
              cm_ln_g, cm_ln_b, cm_w_out, w_mix_out, xa_norm, xa_wq, xa_wo, ffn2_norm, ffn2_wi, ffn2_wo):
    new_ssd, new_ssm, new_cm = [], [], []
    for i in range(DEPTH):
        x = x + 0.5 * swiglu_ffn(rmsnorm(x, ffn1_norm[i]), ffn1_wi[i], ffn1_wo[i])
        mix, s_conv, s_h, s_cm = parallel_mixer(
            rmsnorm(x, mix_norm[i]), ssd_prev[i], ssm_prev[i], cm_prev[i], w_in[i], ssd_conv_w[i], ssd_conv_b[i],
            ssd_dt_bias[i], ssd_a_log[i], ssd_d[i], ssd_norm[i], ssd_w_out[i], cm_dw_w[i], cm_dw_b[i],
            cm_ln_g[i], cm_ln_b[i], cm_w_out[i], w_mix_out[i])
        x = x + mix
        x = x + mem_attend(rmsnorm(x, xa_norm[i]), mem_k[i], mem_v[i], xa_wq[i], xa_wo[i])
        x = x + 0.5 * swiglu_ffn(rmsnorm(x, ffn2_norm[i]), ffn2_wi[i], ffn2_wo[i])
        new_ssd.append(s_conv)
        new_ssm.append(s_h)
        new_cm.append(s_cm)
    return x, jnp.stack(new_ssd), jnp.stack(new_ssm), jnp.stack(new_cm)


def setup_inputs(seed: int = 0) -> dict:
    key = jax.random.key(seed)
    ks = iter(jax.random.split(key, 48))
    f32 = jnp.float32

    def normal(shape, scale):
        return scale * jax.random.normal(next(ks), shape, f32)

    def gain(shape):
        return 1.0 + normal(shape, 0.05)

    dt0 = jnp.exp(jax.random.uniform(next(ks), (DEPTH, SSD_HEADS), f32, math.log(1e-3), math.log(1e-1)))
    return {
        "x_prompt": normal((BATCH, SEQ, D_MODEL), 1.0),
        "x_sample": normal((DEC_BATCH, DEC_SEQ, D_MODEL), 1.0),
        "mem_prompt": normal((BATCH, N_MEM, D_MODEL), 1.0),
        "cache_ssd_conv": normal((DEPTH, DEC_BATCH, SSD_CONV - 1, XBC_DIM), 1.0),
        "cache_ssm_state": normal((DEPTH, DEC_BATCH, SSD_HEADS, SSD_HEADDIM, SSD_STATE), 0.3),
        "cache_cm_conv": normal((DEPTH, DEC_BATCH, CM_WIDTH - 1, D_CONV), 1.0),
        "cache_mem_k": normal((DEPTH, DEC_BATCH, N_MEM, XA_HEADS, XA_HEAD_DIM), 1.0),
        "cache_mem_v": normal((DEPTH, DEC_BATCH, N_MEM, XA_HEADS, XA_HEAD_DIM), 1.0),
        "ffn1_norm": gain((DEPTH, D_MODEL)),
        "ffn1_wi": normal((DEPTH, D_MODEL, 2 * D_FF), D_MODEL ** -0.5),
        "ffn1_wo": normal((DEPTH, D_FF, D_MODEL), D_FF ** -0.5),
        "mix_norm": gain((DEPTH, D_MODEL)),
        "w_in": normal((DEPTH, D_MODEL, IN_COLS), D_MODEL ** -0.5),
        "ssd_conv_w": normal((DEPTH, SSD_CONV, XBC_DIM), SSD_CONV ** -0.5),
        "ssd_conv_b": normal((DEPTH, XBC_DIM), 0.02),
        "ssd_dt_bias": dt0 + jnp.log(-jnp.expm1(-dt0)),
        "ssd_a_log": jnp.log(jax.random.uniform(next(ks), (DEPTH, SSD_HEADS), f32, 1.0, 16.0)),
        "ssd_d": gain((DEPTH, SSD_HEADS)),
        "ssd_norm": gain((DEPTH, D_INNER)),
        "ssd_w_out": normal((DEPTH, D_INNER, D_MODEL), D_INNER ** -0.5),
        "cm_dw_w": normal((DEPTH, CM_WIDTH, D_CONV), CM_WIDTH ** -0.5),
        "cm_dw_b": normal((DEPTH, D_CONV), 0.02),
        "cm_ln_g": gain((DEPTH, D_CONV)),
        "cm_ln_b": normal((DEPTH, D_CONV), 0.02),
        "cm_w_out": normal((DEPTH, D_CONV, D_MODEL), D_CONV ** -0.5),
        "w_mix_out": normal((DEPTH, D_MODEL, D_MODEL), D_MODEL ** -0.5),
        "xa_norm": gain((DEPTH, D_MODEL)),
        "mem_norm": gain((DEPTH, D_MODEL)),
        "xa_wq": normal((DEPTH, D_MODEL, D_MODEL), D_MODEL ** -0.5),
        "xa_wk": normal((DEPTH, D_MODEL, D_MODEL), D_MODEL ** -0.5),
        "xa_wv": normal((DEPTH, D_MODEL, D_MODEL), D_MODEL ** -0.5),
        "xa_wo": normal((DEPTH, D_MODEL, D_MODEL), D_MODEL ** -0.5),
        "ffn2_norm": gain((DEPTH, D_MODEL)),
        "ffn2_wi": normal((DEPTH, D_MODEL, 2 * D_FF), D_MODEL ** -0.5),
        "ffn2_wo": normal((DEPTH, D_FF, D_MODEL), D_FF ** -0.5),
        "final_norm": gain((D_MODEL,)),
    }


def reference(x_prompt, x_sample, mem_prompt, cache_ssd_conv, cache_ssm_state, cache_cm_conv, cache_mem_k,
              cache_mem_v, ffn1_norm, ffn1_wi, ffn1_wo, mix_norm, w_in, ssd_conv_w, ssd_conv_b, ssd_dt_bias,
              ssd_a_log, ssd_d, ssd_norm, ssd_w_out, cm_dw_w, cm_dw_b, cm_ln_g, cm_ln_b, cm_w_out, w_mix_out,
              xa_norm, mem_norm, xa_wq, xa_wk, xa_wv, xa_wo, ffn2_norm, ffn2_wi, ffn2_wo, final_norm):
    p_mem_k = jnp.stack([mem_kv(mem_prompt, mem_norm[i], xa_wk[i]) for i in range(DEPTH)])
    p_mem_v = jnp.stack([mem_kv(mem_prompt, mem_norm[i], xa_wv[i]) for i in range(DEPTH)])
    bp, dtp = x_prompt.shape[0], x_prompt.dtype
    zero_ssd = jnp.zeros((DEPTH, bp, SSD_CONV - 1, XBC_DIM), dtp)
    zero_ssm = jnp.zeros((DEPTH, bp, SSD_HEADS, SSD_HEADDIM, SSD_STATE), dtp)
    zero_cm = jnp.zeros((DEPTH, bp, CM_WIDTH - 1, D_CONV), dtp)
    h_p, p_ssd_conv, p_ssm_state, p_cm_conv = run_trunk(
        x_prompt, zero_ssd, zero_ssm, zero_cm, p_mem_k, p_mem_v, ffn1_norm, ffn1_wi, ffn1_wo, mix_norm, w_in,
        ssd_conv_w, ssd_conv_b, ssd_dt_bias, ssd_a_log, ssd_d, ssd_norm, ssd_w_out, cm_dw_w, cm_dw_b,
        cm_ln_g, cm_ln_b, cm_w_out, w_mix_out, xa_norm, xa_wq, xa_wo, ffn2_norm, ffn2_wi, ffn2_wo)
    y_prompt = rmsnorm(h_p, final_norm)
    h_s, s_ssd_conv, s_ssm_state, s_cm_conv = run_trunk(
        x_sample, cache_ssd_conv, cache_ssm_state, cache_cm_conv, cache_mem_k, cache_mem_v, ffn1_norm, ffn1_wi,
        ffn1_wo, mix_norm, w_in, ssd_conv_w, ssd_conv_b, ssd_dt_bias, ssd_a_log, ssd_d, ssd_norm, ssd_w_out,
        cm_dw_w, cm_dw_b, cm_ln_g, cm_ln_b, cm_w_out, w_mix_out, xa_norm, xa_wq, xa_wo, ffn2_norm, ffn2_wi,
        ffn2_wo)
    y_sample = rmsnorm(h_s, final_norm)
    return (y_prompt, y_sample, p_ssd_conv, p_ssm_state, p_cm_conv, p_mem_k, p_mem_v, s_ssd_conv, s_ssm_state, s_cm_conv)
```

```python
import functools

import jax
import jax.numpy as jnp
from jax import lax
from jax.experimental import pallas as pl
from jax.experimental.pallas import tpu as pltpu

F32 = jnp.float32
BF16 = jnp.bfloat16

D_MODEL = 2048
D_FF = 4 * D_MODEL
D_INNER = 2 * D_MODEL
HEADDIM = 64
N_HEADS = D_INNER // HEADDIM
N_GROUPS = 8
HEADS_PER_GROUP = N_HEADS // N_GROUPS
D_STATE = 128
GROUP_W = D_INNER // N_GROUPS
BC_W = N_GROUPS * D_STATE
SSD_K = 4
XBC = D_INNER + 2 * BC_W
CM_K = 31
CM_HIST = 32
SSD_HIST = 8
N_MEM = 256
XA_HEADS = 4
XA_DIM = D_MODEL // XA_HEADS
CHUNK = 64
EPS = 1e-6
LANES = 128
MAIN_COLS = 2 * D_INNER + 2 * BC_W + 4 * D_MODEL
VMEM_LIMIT = 56 * 1024 * 1024


def _params(sem):
    return pltpu.CompilerParams(dimension_semantics=sem, vmem_limit_bytes=VMEM_LIMIT)


def _rms(x, g):
    return x * lax.rsqrt(jnp.mean(x * x, axis=-1, keepdims=True) + EPS) * g


def _silu(x):
    return x * jax.nn.sigmoid(x)


def _ffn_body(x_ref, g_ref, wa_ref, wb_ref, wo_ref, *rest, final):
    if final:
        fg_ref, o_ref, h_ref, acc_ref = rest
    else:
        o_ref, h_ref, acc_ref = rest
    j = pl.program_id(1)

    @pl.when(j == 0)
    def _():
        h_ref[...] = _rms(x_ref[...], g_ref[...]).astype(BF16)
        acc_ref[...] = jnp.zeros_like(acc_ref)

    h = h_ref[...]
    a = jnp.dot(h, wa_ref[...], preferred_element_type=F32)
    b = jnp.dot(h, wb_ref[...], preferred_element_type=F32)
    acc_ref[...] += jnp.dot((_silu(a) * b).astype(BF16), wo_ref[...], preferred_element_type=F32)

    @pl.when(j == pl.num_programs(1) - 1)
    def _():
        y = x_ref[...] + 0.5 * acc_ref[...]
        if final:
            y = _rms(y, fg_ref[...])
        o_ref[...] = y


def _ffn(x, g, wi, wo, final_g=None, *, bm=640, bf=512):
    m, d = x.shape
    nf = D_FF // bf
    final = final_g is not None
    in_specs = [
        pl.BlockSpec((bm, d), lambda i, j: (i, 0)),
        pl.BlockSpec((1, d), lambda i, j: (0, 0)),
        pl.BlockSpec((d, bf), lambda i, j: (0, j)),
        pl.BlockSpec((d, bf), lambda i, j: (0, j + nf)),
        pl.BlockSpec((bf, d), lambda i, j: (j, 0)),
    ]
    args = [x, g.reshape(1, d), wi, wi, wo]
    if final:
        in_specs.append(pl.BlockSpec((1, d), lambda i, j: (0, 0)))
        args.append(final_g.reshape(1, d))
    return pl.pallas_call(
        functools.partial(_ffn_body, final=final),
        grid=(m // bm, nf),
        in_specs=in_specs,
        out_specs=pl.BlockSpec((bm, d), lambda i, j: (i, 0)),
        out_shape=jax.ShapeDtypeStruct((m, d), F32),
        scratch_shapes=[pltpu.VMEM((bm, d), BF16), pltpu.VMEM((bm, d), F32)],
        compiler_params=_params(("parallel", "arbitrary")),
        name="ffn",
    )(*args)


def _norm_mm_body(x_ref, g_ref, w_ref, o_ref, h_ref):
    @pl.when(pl.program_id(1) == 0)
    def _():
        h_ref[...] = _rms(x_ref[...], g_ref[...]).astype(BF16)

    o_ref[...] = jnp.dot(h_ref[...], w_ref[...], preferred_element_type=F32).astype(o_ref.dtype)


def _norm_mm(x, g, w, *, bm, bn, out_dtype=F32, name="norm_mm"):
    m, d = x.shape
    n = w.shape[1]
    return pl.pallas_call(
        _norm_mm_body,
        grid=(m // bm, n // bn),
        in_specs=[
            pl.BlockSpec((bm, d), lambda i, j: (i, 0)),
            pl.BlockSpec((1, d), lambda i, j: (0, 0)),
            pl.BlockSpec((d, bn), lambda i, j: (0, j)),
        ],
        out_specs=pl.BlockSpec((bm, bn), lambda i, j: (i, j)),
        out_shape=jax.ShapeDtypeStruct((m, n), out_dtype),
        scratch_shapes=[pltpu.VMEM((bm, d), BF16)],
        compiler_params=_params(("parallel", "arbitrary")),
        name=name,
    )(x, g.reshape(1, d), w)


def _mm_res_body(a_ref, w_ref, r_ref, o_ref):
    o_ref[...] = r_ref[...] + jnp.dot(a_ref[...], w_ref[...], preferred_element_type=F32)


def _mm_res(a, w, res, *, bm, bn, name):
    m, k = a.shape
    n = w.shape[1]
    return pl.pallas_call(
        _mm_res_body,
        grid=(m // bm, n // bn),
        in_specs=[
            pl.BlockSpec((bm, k), lambda i, j: (i, 0)),
            pl.BlockSpec((k, bn), lambda i, j: (0, j)),
            pl.BlockSpec((bm, bn), lambda i, j: (i, j)),
        ],
        out_specs=pl.BlockSpec((bm, bn), lambda i, j: (i, j)),
        out_shape=jax.ShapeDtypeStruct((m, n), F32),
        compiler_params=_params(("parallel", "arbitrary")),
        name=name,
    )(a, w, res)


def _merge_body(y_ref, u_ref, ws_ref, wc_ref, gs_ref, gc_ref, o_ref):
    ssd = jnp.dot(y_ref[...], ws_ref[...], preferred_element_type=F32)
    cm = jnp.dot(u_ref[...], wc_ref[...], preferred_element_type=F32)
    o_ref[...] = (jax.nn.sigmoid(gs_ref[...]) * ssd + jax.nn.sigmoid(gc_ref[...]) * cm).astype(o_ref.dtype)


def _merge(yn, un, w_ssd, w_cm, proj, *, bm=640, bn=512):
    m = yn.shape[0]
    gs_off = (MAIN_COLS - 2 * D_MODEL) // bn
    gc_off = (MAIN_COLS - D_MODEL) // bn
    return pl.pallas_call(
        _merge_body,
        grid=(m // bm, D_MODEL // bn),
        in_specs=[
            pl.BlockSpec((bm, D_INNER), lambda i, j: (i, 0)),
            pl.BlockSpec((bm, D_MODEL), lambda i, j: (i, 0)),
            pl.BlockSpec((D_INNER, bn), lambda i, j: (0, j)),
            pl.BlockSpec((D_MODEL, bn), lambda i, j: (0, j)),
            pl.BlockSpec((bm, bn), lambda i, j: (i, j + gs_off)),
            pl.BlockSpec((bm, bn), lambda i, j: (i, j + gc_off)),
        ],
        out_specs=pl.BlockSpec((bm, bn), lambda i, j: (i, j)),
        out_shape=jax.ShapeDtypeStruct((m, D_MODEL), BF16),
        compiler_params=_params(("parallel", "arbitrary")),
        name="merge",
    )(yn, un, w_ssd, w_cm, proj, proj)


def _split3(v):
    hi = v.astype(BF16)
    r = v - hi.astype(F32)
    mid = r.astype(BF16)
    lo = (r - mid.astype(F32)).astype(BF16)
    return hi, mid, lo


def _ssd_body(z_ref, xs_ref, b_ref, c_ref, dt_ref, wx_ref, wb_ref, wc_ref, bx_ref, bb_ref, bc_ref,
              dtb_ref, alog_ref, dexp_ref, ng_ref, e_ref, px_ref, pb_ref, pc_ref, h0_ref,
              y_ref, h_ref, xbuf, bbuf, cbuf, *, t):
    @pl.when(pl.program_id(1) == 0)
    def _():
        xbuf[0:SSD_HIST, :] = px_ref[0]
        bbuf[0:SSD_HIST, :] = pb_ref[0]
        cbuf[0:SSD_HIST, :] = pc_ref[0]
        h_ref[0] = h0_ref[0]

    def conv_silu(buf, raw_ref, w_ref, bias_ref):
        raw = raw_ref[...]
        buf[SSD_HIST:SSD_HIST + t, :] = raw
        acc = bias_ref[...] + w_ref[SSD_K - 1:SSD_K, :] * raw
        for k in range(SSD_K - 1):
            r0 = SSD_HIST - (SSD_K - 1) + k
            acc = acc + w_ref[k:k + 1, :] * buf[r0:r0 + t, :]
        buf[0:SSD_HIST, :] = buf[t:t + SSD_HIST, :]
        return _silu(acc)

    xs = conv_silu(xbuf, xs_ref, wx_ref, bx_ref)
    bm = conv_silu(bbuf, b_ref, wb_ref, bb_ref)
    cm = conv_silu(cbuf, c_ref, wc_ref, bc_ref)

    dtp = jax.nn.softplus(dt_ref[...] + dtb_ref[...])
    d_a = dtp * (-jnp.exp(alog_ref[...]))
    row = lax.broadcasted_iota(jnp.int32, (t, LANES), 0)
    a_cs = d_a
    s = 1
    while s < t:
        a_cs = a_cs + jnp.where(row >= s, pltpu.roll(a_cs, s, 0), 0.0)
        s *= 2
    a_last = a_cs[t - 1:t, :]
    e_in = jnp.exp(a_cs)
    e_out = jnp.exp(a_last - a_cs)

    stack = jnp.concatenate([dtp, e_in, e_out], axis=0)
    ex = sum(jnp.dot(p, e_ref[...], preferred_element_type=F32) for p in _split3(stack))
    dt_e, ein_e, eout_e = ex[0:t], ex[t:2 * t], ex[2 * t:3 * t]

    xdt = xs * dt_e
    xdt_b = xdt.astype(BF16)
    xw_b = (xdt * eout_e).astype(BF16)

    pad = jnp.zeros((LANES - t, LANES), F32)
    a_cs_t = jnp.concatenate([a_cs, pad], axis=0).T[:, 0:t]
    li = lax.broadcasted_iota(jnp.int32, (t, t), 0)
    si = lax.broadcasted_iota(jnp.int32, (t, t), 1)
    causal = li >= si

    z = z_ref[...]
    nt = (((1,), (1,)), ((), ()))
    tn = (((0,), (0,)), ((), ()))
    for g in range(N_GROUPS):
        gs = slice(g * GROUP_W, (g + 1) * GROUP_W)
        ns = slice(g * D_STATE, (g + 1) * D_STATE)
        c_g = cm[:, ns].astype(BF16)
        b_g = bm[:, ns].astype(BF16)
        cb = lax.dot_general(c_g, b_g, nt, preferred_element_type=F32)
        h_prev = h_ref[0, :, gs]
        y_off = jnp.dot(c_g, h_prev.astype(BF16), preferred_element_type=F32) * ein_e[:, gs]
        y_heads = []
        for r in range(HEADS_PER_GROUP):
            h = g * HEADS_PER_GROUP + r
            seg = a_cs[:, h:h + 1] - a_cs_t[h:h + 1, :]
            m_h = (cb * jnp.where(causal, jnp.exp(seg), 0.0)).astype(BF16)
            y_heads.append(jnp.dot(m_h, xdt_b[:, h * HEADDIM:(h + 1) * HEADDIM], preferred_element_type=F32))
        y = jnp.concatenate(y_heads, axis=1) + y_off
        states = lax.dot_general(b_g, xw_b[:, gs], tn, preferred_element_type=F32)
        h_ref[0, :, gs] = h_prev * ein_e[t - 1:t, gs] + states

        y = y + dexp_ref[:, gs] * xs[:, gs]
        v = y * _silu(z[:, gs])
        y_ref[:, gs] = _rms(v, ng_ref[:, gs]).astype(y_ref.dtype)


def _ssd(proj, pdt, row0, nb, seq, conv_w, conv_b, dt_bias, a_log, d_exp, norm_g, expand, prev, h0t):
    t = min(CHUNK, seq)
    nc = seq // t
    rb = row0 // t
    xi, bi, ci = D_INNER // D_INNER, (2 * D_INNER) // BC_W, (2 * D_INNER + BC_W) // BC_W
    rows = lambda b, c: rb + b * nc + c
    full = lambda shape: pl.BlockSpec(shape, lambda b, c: (0, 0))
    in_specs = [
        pl.BlockSpec((t, D_INNER), lambda b, c: (rows(b, c), 0)),
        pl.BlockSpec((t, D_INNER), lambda b, c: (rows(b, c), xi)),
        pl.BlockSpec((t, BC_W), lambda b, c: (rows(b, c), bi)),
        pl.BlockSpec((t, BC_W), lambda b, c: (rows(b, c), ci)),
        pl.BlockSpec((t, LANES), lambda b, c: (rows(b, c), 0)),
        pl.BlockSpec((SSD_K, D_INNER), lambda b, c: (0, 0)),
        pl.BlockSpec((SSD_K, BC_W), lambda b, c: (0, D_INNER // BC_W)),
        pl.BlockSpec((SSD_K, BC_W), lambda b, c: (0, D_INNER // BC_W + 1)),
        pl.BlockSpec((1, D_INNER), lambda b, c: (0, 0)),
        pl.BlockSpec((1, BC_W), lambda b, c: (0, D_INNER // BC_W)),
        pl.BlockSpec((1, BC_W), lambda b, c: (0, D_INNER // BC_W + 1)),
        full((1, LANES)), full((1, LANES)), full((1, D_INNER)), full((1, D_INNER)), full((LANES, D_INNER)),
        pl.BlockSpec((1, SSD_HIST, D_INNER), lambda b, c: (b, 0, 0)),
        pl.BlockSpec((1, SSD_HIST, BC_W), lambda b, c: (b, 0, D_INNER // BC_W)),
        pl.BlockSpec((1, SSD_HIST, BC_W), lambda b, c: (b, 0, D_INNER // BC_W + 1)),
        pl.BlockSpec((1, D_STATE, D_INNER), lambda b, c: (b, 0, 0)),
    ]
    return pl.pallas_call(
        functools.partial(_ssd_body, t=t),
        grid=(nb, nc),
        in_specs=in_specs,
        out_specs=[
            pl.BlockSpec((t, D_INNER), lambda b, c: (b * nc + c, 0)),
            pl.BlockSpec((1, D_STATE, D_INNER), lambda b, c: (b, 0, 0)),
        ],
        out_shape=[
            jax.ShapeDtypeStruct((nb * seq, D_INNER), BF16),
            jax.ShapeDtypeStruct((nb, D_STATE, D_INNER), F32),
        ],
        scratch_shapes=[
            pltpu.VMEM((SSD_HIST + t, D_INNER), F32),
            pltpu.VMEM((SSD_HIST + t, BC_W), F32),
            pltpu.VMEM((SSD_HIST + t, BC_W), F32),
        ],
        compiler_params=_params(("parallel", "arbitrary")),
        name="ssd",
    )(proj, proj, proj, proj, pdt, conv_w, conv_w, conv_w, conv_b, conv_b, conv_b,
      dt_bias, a_log, d_exp, norm_g, expand, prev, prev, prev, h0t)


CM_COLS = 256
CM_ROWS = 64


def _cm_body(v_ref, g_ref, w_ref, b_ref, lg_ref, lb_ref, p_ref, o_ref, s_ref, buf, ybuf, *, t):
    @pl.when(pl.program_id(1) == 0)
    def _():
        buf[0:CM_HIST, :] = p_ref[0]

    buf[CM_HIST:CM_HIST + t, :] = v_ref[...] * jax.nn.sigmoid(g_ref[...])
    base = CM_HIST - (CM_K - 1)
    rt = min(CM_ROWS, t)

    def cols(ci, carry):
        cs = pl.ds(pl.multiple_of(ci * CM_COLS, CM_COLS), CM_COLS)
        for r0 in range(0, t, rt):
            acc = jnp.broadcast_to(b_ref[:, cs], (rt, CM_COLS))
            for k in range(CM_K):
                acc = acc + w_ref[k:k + 1, cs] * buf[base + k + r0:base + k + r0 + rt, cs]
            ybuf[r0:r0 + rt, cs] = acc
        return carry

    lax.fori_loop(0, D_MODEL // CM_COLS, cols, 0)

    hist = buf[t:t + CM_HIST, :]
    buf[0:CM_HIST, :] = hist

    @pl.when(pl.program_id(1) == pl.num_programs(1) - 1)
    def _():
        s_ref[0] = hist

    y = ybuf[...]
    mu = jnp.mean(y, axis=-1, keepdims=True)
    yc = y - mu
    var = jnp.mean(yc * yc, axis=-1, keepdims=True)
    o_ref[...] = _silu(yc * lax.rsqrt(var + EPS) * lg_ref[...] + lb_ref[...]).astype(o_ref.dtype)


def _convmod(proj, row0, nb, seq, dw_w, dw_b, ln_g, ln_b, prev):
    t = min(256, seq)
    nc = seq // t
    rb = row0 // t
    vi = (2 * D_INNER + 2 * BC_W) // D_MODEL
    rows = lambda b, c: rb + b * nc + c
    full = lambda shape: pl.BlockSpec(shape, lambda b, c: (0, 0))
    return pl.pallas_call(
        functools.partial(_cm_body, t=t),
        grid=(nb, nc),
        in_specs=[
            pl.BlockSpec((t, D_MODEL), lambda b, c: (rows(b, c), vi)),
            pl.BlockSpec((t, D_MODEL), lambda b, c: (rows(b, c), vi + 1)),
            full((CM_K, D_MODEL)), full((1, D_MODEL)), full((1, D_MODEL)), full((1, D_MODEL)),
            pl.BlockSpec((1, CM_HIST, D_MODEL), lambda b, c: (b, 0, 0)),
        ],
        out_specs=[
            pl.BlockSpec((t, D_MODEL), lambda b, c: (b * nc + c, 0)),
            pl.BlockSpec((1, CM_HIST, D_MODEL), lambda b, c: (b, 0, 0)),
        ],
        out_shape=[
            jax.ShapeDtypeStruct((nb * seq, D_MODEL), BF16),
            jax.ShapeDtypeStruct((nb, CM_HIST, D_MODEL), F32),
        ],
        scratch_shapes=[pltpu.VMEM((CM_HIST + t, D_MODEL), F32), pltpu.VMEM((t, D_MODEL), F32)],
        compiler_params=_params(("parallel", "arbitrary")),
        name="convmod",
    )(proj, proj, dw_w, dw_b, ln_g, ln_b, prev)


def _attn_body(q_ref, k_ref, v_ref, o_ref, kb, vb):
    @pl.when(pl.program_id(1) == 0)
    def _():
        kb[...] = k_ref[0].astype(BF16)
        vb[...] = v_ref[0].astype(BF16)

    nt = (((1,), (1,)), ((), ()))
    for h in range(XA_HEADS):
        hs = slice(h * XA_DIM, (h + 1) * XA_DIM)
        s = lax.dot_general(q_ref[:, hs], kb[:, hs], nt, preferred_element_type=F32) * (XA_DIM ** -0.5)
        e = jnp.exp(s - jnp.max(s, axis=-1, keepdims=True))
        p = (e / jnp.sum(e, axis=-1, keepdims=True)).astype(BF16)
        o_ref[:, hs] = jnp.dot(p, vb[:, hs], preferred_element_type=F32).astype(o_ref.dtype)


def _attend(q, row0, nb, seq, k, v):
    tq = min(512, seq)
    nq = seq // tq
    rb = row0 // tq
    return pl.pallas_call(
        _attn_body,
        grid=(nb, nq),
        in_specs=[
            pl.BlockSpec((tq, D_MODEL), lambda b, i: (rb + b * nq + i, 0)),
            pl.BlockSpec((1, N_MEM, D_MODEL), lambda b, i: (b, 0, 0)),
            pl.BlockSpec((1, N_MEM, D_MODEL), lambda b, i: (b, 0, 0)),
        ],
        out_specs=pl.BlockSpec((tq, D_MODEL), lambda b, i: (b * nq + i, 0)),
        out_shape=jax.ShapeDtypeStruct((nb * seq, D_MODEL), BF16),
        scratch_shapes=[pltpu.VMEM((N_MEM, D_MODEL), BF16), pltpu.VMEM((N_MEM, D_MODEL), BF16)],
        compiler_params=_params(("parallel", "arbitrary")),
        name="attend",
    )(q, k, v)


def _pad_rows_front(a, rows):
    return jnp.pad(a, ((0, 0), (rows - a.shape[1], 0), (0, 0)))


def _state_to_kernel(h):
    b = h.shape[0]
    return jnp.transpose(h.reshape(b, D_INNER, D_STATE), (0, 2, 1))


def _state_from_kernel(ht):
    b = ht.shape[0]
    return jnp.transpose(ht, (0, 2, 1)).reshape(b, N_HEADS, HEADDIM, D_STATE)


def kernel(x_prompt, x_sample, mem_prompt, cache_ssd_conv, cache_ssm_state, cache_cm_conv, cache_mem_k, cache_mem_v, ffn1_norm, ffn1_wi, ffn1_wo, mix_norm, w_in, ssd_conv_w, ssd_conv_b, ssd_dt_bias, ssd_a_log, ssd_d, ssd_norm, ssd_w_out, cm_dw_w, cm_dw_b, cm_ln_g, cm_ln_b, cm_w_out, w_mix_out, xa_norm, mem_norm, xa_wq, xa_wk, xa_wv, xa_wo, ffn2_norm, ffn2_wi, ffn2_wo, final_norm):
    assert ffn1_wi.shape[0] == 1, "single layer"
    bp, lp, _ = x_prompt.shape
    bs, ls, _ = x_sample.shape
    np_rows, ns_rows = bp * lp, bs * ls
    x0 = jnp.concatenate([x_prompt.reshape(np_rows, D_MODEL), x_sample.reshape(ns_rows, D_MODEL)], axis=0)

    i1 = D_INNER
    i2 = i1 + XBC
    i3 = i2 + N_HEADS
    w = w_in[0]
    w_main = jnp.concatenate([w[:, :i2], w[:, i3:]], axis=1).astype(BF16)
    w_dt = jnp.pad(w[:, i2:i3], ((0, 0), (0, LANES - N_HEADS))).astype(BF16)
    cast = lambda a: a[0].astype(BF16)

    x1 = _ffn(x0, ffn1_norm[0], cast(ffn1_wi), cast(ffn1_wo))
    proj = _norm_mm(x1, mix_norm[0], w_main, bm=1040, bn=1024, name="in_proj")
    pdt = _norm_mm(x1, mix_norm[0], w_dt, bm=1040, bn=LANES, name="dt_proj")

    pad_h = lambda a: jnp.pad(a[0].reshape(1, N_HEADS), ((0, 0), (0, LANES - N_HEADS)))
    d_exp = jnp.repeat(ssd_d[0], HEADDIM).reshape(1, D_INNER)
    expand = (lax.broadcasted_iota(jnp.int32, (LANES, D_INNER), 1) // HEADDIM
              == lax.broadcasted_iota(jnp.int32, (LANES, D_INNER), 0)).astype(BF16)
    ssd_args = (ssd_conv_w[0], ssd_conv_b[0].reshape(1, XBC), pad_h(ssd_dt_bias), pad_h(ssd_a_log), d_exp,
                ssd_norm[0].reshape(1, D_INNER), expand)
    yn_p, hp = _ssd(proj, pdt, 0, bp, lp, *ssd_args,
                    jnp.zeros((bp, SSD_HIST, XBC), F32), jnp.zeros((bp, D_STATE, D_INNER), F32))
    yn_s, hs = _ssd(proj, pdt, np_rows, bs, ls, *ssd_args,
                    _pad_rows_front(cache_ssd_conv[0], SSD_HIST), _state_to_kernel(cache_ssm_state[0]))
    xbc_cols = slice(D_INNER, D_INNER + XBC)
    p_ssd_conv = proj[:np_rows, xbc_cols].reshape(bp, lp, XBC)[:, lp - (SSD_K - 1):]
    s_ssd_conv = jnp.concatenate(
        [cache_ssd_conv[0], proj[np_rows:, xbc_cols].reshape(bs, ls, XBC)], axis=1)[:, -(SSD_K - 1):]

    cm_args = (cm_dw_w[0], cm_dw_b[0].reshape(1, D_MODEL), cm_ln_g[0].reshape(1, D_MODEL),
               cm_ln_b[0].reshape(1, D_MODEL))
    un_p, cp = _convmod(proj, 0, bp, lp, *cm_args, jnp.zeros((bp, CM_HIST, D_MODEL), F32))
    un_s, cs = _convmod(proj, np_rows, bs, ls, *cm_args, _pad_rows_front(cache_cm_conv[0], CM_HIST))

    m = _merge(jnp.concatenate([yn_p, yn_s], axis=0), jnp.concatenate([un_p, un_s], axis=0),
               cast(ssd_w_out), cast(cm_w_out), proj)
    x2 = _mm_res(m, cast(w_mix_out), x1, bm=1040, bn=1024, name="mix_out")

    mem = mem_prompt.reshape(bp * N_MEM, D_MODEL)
    k_p = _norm_mm(mem, mem_norm[0], cast(xa_wk), bm=bp * N_MEM, bn=1024, name="mem_k")
    v_p = _norm_mm(mem, mem_norm[0], cast(xa_wv), bm=bp * N_MEM, bn=1024, name="mem_v")
    q = _norm_mm(x2, xa_norm[0], cast(xa_wq), bm=1040, bn=1024, out_dtype=BF16, name="q_proj")
    o_p = _attend(q, 0, bp, lp, k_p.reshape(bp, N_MEM, D_MODEL), v_p.reshape(bp, N_MEM, D_MODEL))
    o_s = _attend(q, np_rows, bs, ls, cache_mem_k[0].reshape(bs, N_MEM, D_MODEL),
                  cache_mem_v[0].reshape(bs, N_MEM, D_MODEL))
    x3 = _mm_res(jnp.concatenate([o_p, o_s], axis=0), cast(xa_wo), x2, bm=1040, bn=1024, name="attn_out")

    y = _ffn(x3, ffn2_norm[0], cast(ffn2_wi), cast(ffn2_wo), final_g=final_norm)

    kv_shape = (1, bp, N_MEM, XA_HEADS, XA_DIM)
    return (
        y[:np_rows].reshape(bp, lp, D_MODEL),
        y[np_rows:].reshape(bs, ls, D_MODEL),
        p_ssd_conv[None],
        _state_from_kernel(hp)[None],
        cp[:, CM_HIST - (CM_K - 1):][None],
        k_p.reshape(kv_shape),
        v_p.reshape(kv_shape),
        s_ssd_conv[None],
        _state_from_kernel(hs)[None],
        cs[:, CM_HIST - (CM_K - 1):][None],
    )
```

```python
import functools

import jax
import jax.numpy as jnp
from jax import lax
from jax.experimental import pallas as pl
from jax.experimental.pallas import tpu as pltpu

F32 = jnp.float32
BF16 = jnp.bfloat16

D_MODEL = 2048
D_FF = 4 * D_MODEL
D_INNER = 2 * D_MODEL
HEADDIM = 64
N_HEADS = D_INNER // HEADDIM
N_GROUPS = 8
HEADS_PER_GROUP = N_HEADS // N_GROUPS
D_STATE = 128
GROUP_W = D_INNER // N_GROUPS
BC_W = N_GROUPS * D_STATE
SSD_K = 4
XBC = D_INNER + 2 * BC_W
CM_K = 31
CM_HIST = 32
SSD_HIST = 8
N_MEM = 256
XA_HEADS = 4
XA_DIM = D_MODEL // XA_HEADS
CHUNK = 64
EPS = 1e-6
LANES = 128
SUBLANES = 8
MAIN_COLS = 2 * D_INNER + 2 * BC_W + 4 * D_MODEL
VMEM_LIMIT = 56 * 1024 * 1024


def _params(sem):
    return pltpu.CompilerParams(dimension_semantics=sem, vmem_limit_bytes=VMEM_LIMIT)


def _rms(x, g):
    return x * lax.rsqrt(jnp.mean(x * x, axis=-1, keepdims=True) + EPS) * g


def _sigmoid(x):
    return 0.5 * jnp.tanh(0.5 * x) + 0.5


def _silu(x):
    return x * _sigmoid(x)


def _on_row_tiles(i, rows_main, full_fn, split_fn):
    last = pl.num_programs(0) - 1
    pl.when(i != last)(full_fn)
    pl.when(i == last)(lambda: split_fn(rows_main))


def _ffn_body(*refs, final, tail, split, r):
    it = iter(refs)
    x_ref = next(it)
    xt_ref = next(it) if tail else None
    g_ref, wa_ref, wb_ref, wo_ref = next(it), next(it), next(it), next(it)
    fg_ref = next(it) if final else None
    o_ref = next(it)
    ot_ref = next(it) if split else None
    h_ref, acc_ref = next(it), next(it)
    i, j = pl.program_id(0), pl.program_id(1)

    def norm_in(x):
        return _rms(x, g_ref[...]).astype(BF16)

    def finish(x, acc):
        y = x + 0.5 * acc
        return _rms(y, fg_ref[...]) if final else y

    @pl.when(j == 0)
    def _():
        acc_ref[...] = jnp.zeros_like(acc_ref)
        if tail:
            def full():
                h_ref[...] = norm_in(x_ref[...])

            def last(rm):
                h_ref[0:rm, :] = norm_in(x_ref[0:rm, :])
                h_ref[rm:, :] = norm_in(xt_ref[...])

            _on_row_tiles(i, r, full, last)
        else:
            h_ref[...] = norm_in(x_ref[...])

    h = h_ref[...]
    a = jnp.dot(h, wa_ref[...], preferred_element_type=F32)
    b = jnp.dot(h, wb_ref[...], preferred_element_type=F32)
    acc_ref[...] += jnp.dot((_silu(a) * b).astype(BF16), wo_ref[...], preferred_element_type=F32)

    @pl.when(j == pl.num_programs(1) - 1)
    def _():
        if tail or split:
            def full():
                o_ref[...] = finish(x_ref[...], acc_ref[...])

            def last(rm):
                o_ref[0:rm, :] = finish(x_ref[0:rm, :], acc_ref[0:rm, :])
                yt = finish(xt_ref[...] if tail else x_ref[rm:, :], acc_ref[rm:, :])
                if split:
                    ot_ref[...] = yt
                else:
                    o_ref[rm:, :] = yt

            _on_row_tiles(i, r, full, last)
        else:
            o_ref[...] = finish(x_ref[...], acc_ref[...])


def _ffn(x, x_tail, g, wi, wo, final_g=None, split_rows=None, *, bm=640, bf=512):
    d = x.shape[1]
    tail, split, final = x_tail is not None, split_rows is not None, final_g is not None
    m = x.shape[0] + (x_tail.shape[0] if tail else 0)
    n_main = x.shape[0] if tail else (split_rows if split else m)
    nm, nf = m // bm, D_FF // bf
    r = n_main - (nm - 1) * bm
    assert nm * bm == m and 0 < r <= bm and (r == bm or m - n_main == bm - r)
    row = lambda i, j: (i, 0)
    const = lambda i, j: (0, 0)
    in_specs = [pl.BlockSpec((bm, d), row)]
    args = [x]
    if tail:
        in_specs.append(pl.BlockSpec((bm - r, d), const))
        args.append(x_tail)
    in_specs += [
        pl.BlockSpec((1, d), const),
        pl.BlockSpec((d, bf), lambda i, j: (0, j)),
        pl.BlockSpec((d, bf), lambda i, j: (0, j + nf)),
        pl.BlockSpec((bf, d), lambda i, j: (j, 0)),
    ]
    args += [g.reshape(1, d), wi, wi, wo]
    if final:
        in_specs.append(pl.BlockSpec((1, d), const))
        args.append(final_g.reshape(1, d))
    if split:
        out_specs = [pl.BlockSpec((bm, d), row), pl.BlockSpec((bm - r, d), const)]
        out_shape = [jax.ShapeDtypeStruct((n_main, d), F32), jax.ShapeDtypeStruct((m - n_main, d), F32)]
    else:
        out_specs = pl.BlockSpec((bm, d), row)
        out_shape = jax.ShapeDtypeStruct((m, d), F32)
    return pl.pallas_call(
        functools.partial(_ffn_body, final=final, tail=tail, split=split, r=r),
        grid=(nm, nf),
        in_specs=in_specs,
        out_specs=out_specs,
        out_shape=out_shape,
        scratch_shapes=[pltpu.VMEM((bm, d), BF16), pltpu.VMEM((bm, d), F32)],
        compiler_params=_params(("arbitrary", "arbitrary")),
        name="ffn",
    )(*args)


def _norm_mm_body(x_ref, g_ref, w_ref, o_ref, h_ref):
    @pl.when(pl.program_id(1) == 0)
    def _():
        h_ref[...] = _rms(x_ref[...], g_ref[...]).astype(BF16)

    o_ref[...] = jnp.dot(h_ref[...], w_ref[...], preferred_element_type=F32).astype(o_ref.dtype)


def _norm_mm(x, g, w, *, bm, bn, out_dtype=F32, name="norm_mm"):
    m, d = x.shape
    n = w.shape[1]
    return pl.pallas_call(
        _norm_mm_body,
        grid=(m // bm, n // bn),
        in_specs=[
            pl.BlockSpec((bm, d), lambda i, j: (i, 0)),
            pl.BlockSpec((1, d), lambda i, j: (0, 0)),
            pl.BlockSpec((d, bn), lambda i, j: (0, j)),
        ],
        out_specs=pl.BlockSpec((bm, bn), lambda i, j: (i, j)),
        out_shape=jax.ShapeDtypeStruct((m, n), out_dtype),
        scratch_shapes=[pltpu.VMEM((bm, d), BF16)],
        compiler_params=_params(("parallel", "arbitrary")),
        name=name,
    )(x, g.reshape(1, d), w)


def _mm_res_body(a_ref, *rest, tail, r):
    at_ref = rest[0] if tail else None
    w_ref, r_ref, o_ref = rest[-3:]

    def tile(a, res):
        return res + jnp.dot(a, w_ref[...], preferred_element_type=F32)

    if tail:
        def full():
            o_ref[...] = tile(a_ref[...], r_ref[...])

        def last(rm):
            o_ref[0:rm, :] = tile(a_ref[0:rm, :], r_ref[0:rm, :])
            o_ref[rm:, :] = tile(at_ref[...], r_ref[rm:, :])

        _on_row_tiles(pl.program_id(0), r, full, last)
    else:
        o_ref[...] = tile(a_ref[...], r_ref[...])


def _mm_res(a, a_tail, w, res, *, bm, bn, name):
    k = a.shape[1]
    m, n = res.shape
    tail = a_tail is not None
    nm = m // bm
    r = a.shape[0] - (nm - 1) * bm
    assert nm * bm == m and (not tail or a_tail.shape[0] == bm - r)
    in_specs = [pl.BlockSpec((bm, k), lambda i, j: (i, 0))]
    args = [a]
    if tail:
        in_specs.append(pl.BlockSpec((bm - r, k), lambda i, j: (0, 0)))
        args.append(a_tail)
    in_specs += [pl.BlockSpec((k, bn), lambda i, j: (0, j)), pl.BlockSpec((bm, bn), lambda i, j: (i, j))]
    return pl.pallas_call(
        functools.partial(_mm_res_body, tail=tail, r=r),
        grid=(nm, n // bn),
        in_specs=in_specs,
        out_specs=pl.BlockSpec((bm, bn), lambda i, j: (i, j)),
        out_shape=jax.ShapeDtypeStruct((m, n), F32),
        compiler_params=_params(("parallel", "arbitrary")),
        name=name,
    )(*args, w, res)


def _merge_body(y_ref, yt_ref, u_ref, ut_ref, ws_ref, wc_ref, gs_ref, gc_ref, o_ref, *, r):
    def tile(y, u, gs, gc):
        ssd = jnp.dot(y, ws_ref[...], preferred_element_type=F32)
        cm = jnp.dot(u, wc_ref[...], preferred_element_type=F32)
        return (_sigmoid(gs) * ssd + _sigmoid(gc) * cm).astype(o_ref.dtype)

    def full():
        o_ref[...] = tile(y_ref[...], u_ref[...], gs_ref[...], gc_ref[...])

    def last(rm):
        o_ref[0:rm, :] = tile(y_ref[0:rm, :], u_ref[0:rm, :], gs_ref[0:rm, :], gc_ref[0:rm, :])
        o_ref[rm:, :] = tile(yt_ref[...], ut_ref[...], gs_ref[rm:, :], gc_ref[rm:, :])

    _on_row_tiles(pl.program_id(0), r, full, last)


def _merge(yn, yn_tail, un, un_tail, w_ssd, w_cm, proj, *, bm=640, bn=512):
    m = proj.shape[0]
    nm = m // bm
    r = yn.shape[0] - (nm - 1) * bm
    assert nm * bm == m and yn_tail.shape[0] == bm - r
    gs_off = (MAIN_COLS - 2 * D_MODEL) // bn
    gc_off = (MAIN_COLS - D_MODEL) // bn
    return pl.pallas_call(
        functools.partial(_merge_body, r=r),
        grid=(nm, D_MODEL // bn),
        in_specs=[
            pl.BlockSpec((bm, D_INNER), lambda i, j: (i, 0)),
            pl.BlockSpec((bm - r, D_INNER), lambda i, j: (0, 0)),
            pl.BlockSpec((bm, D_MODEL), lambda i, j: (i, 0)),
            pl.BlockSpec((bm - r, D_MODEL), lambda i, j: (0, 0)),
            pl.BlockSpec((D_INNER, bn), lambda i, j: (0, j)),
            pl.BlockSpec((D_MODEL, bn), lambda i, j: (0, j)),
            pl.BlockSpec((bm, bn), lambda i, j: (i, j + gs_off)),
            pl.BlockSpec((bm, bn), lambda i, j: (i, j + gc_off)),
        ],
        out_specs=pl.BlockSpec((bm, bn), lambda i, j: (i, j)),
        out_shape=jax.ShapeDtypeStruct((m, D_MODEL), BF16),
        compiler_params=_params(("parallel", "arbitrary")),
        name="merge",
    )(yn, yn_tail, un, un_tail, w_ssd, w_cm, proj, proj)


def _split3(v):
    hi = v.astype(BF16)
    r = v - hi.astype(F32)
    mid = r.astype(BF16)
    lo = (r - mid.astype(F32)).astype(BF16)
    return hi, mid, lo


def _ssd_body(z_ref, xs_ref, b_ref, c_ref, dt_ref, wx_ref, wb_ref, wc_ref, bx_ref, bb_ref, bc_ref,
              dtb_ref, alog_ref, dexp_ref, ng_ref, e_ref, px_ref, pb_ref, pc_ref, h0_ref,
              y_ref, hout_ref, xbuf, bbuf, cbuf, h_ref, *, t):
    @pl.when(pl.program_id(1) == 0)
    def _():
        xbuf[0:SSD_HIST, :] = px_ref[0]
        bbuf[0:SSD_HIST, :] = pb_ref[0]
        cbuf[0:SSD_HIST, :] = pc_ref[0]
        for i in range(D_INNER // LANES):
            cs = slice(i * LANES, (i + 1) * LANES)
            h_ref[:, cs] = h0_ref[0, cs, :].T

    def conv_silu(buf, raw_ref, w_ref, bias_ref):
        raw = raw_ref[...]
        buf[SSD_HIST:SSD_HIST + t, :] = raw
        win = buf[...]
        acc = bias_ref[...] + w_ref[SSD_K - 1:SSD_K, :] * raw
        for k in range(SSD_K - 1):
            r0 = SSD_HIST - (SSD_K - 1) + k
            acc = acc + w_ref[k:k + 1, :] * pltpu.roll(win, SSD_HIST + t - r0, 0)[0:t, :]
        buf[0:SSD_HIST, :] = win[t:t + SSD_HIST, :]
        return _silu(acc)

    xs = conv_silu(xbuf, xs_ref, wx_ref, bx_ref)
    bm = conv_silu(bbuf, b_ref, wb_ref, bb_ref)
    cm = conv_silu(cbuf, c_ref, wc_ref, bc_ref)

    dtp = jax.nn.softplus(dt_ref[...] + dtb_ref[...])
    d_a = dtp * (-jnp.exp(alog_ref[...]))
    row = lax.broadcasted_iota(jnp.int32, (t, LANES), 0)
    a_cs = d_a
    s = 1
    while s < t:
        a_cs = a_cs + jnp.where(row >= s, pltpu.roll(a_cs, s, 0), 0.0)
        s *= 2
    a_last = a_cs[t - 1:t, :]
    e_in = jnp.exp(a_cs)
    e_out = jnp.exp(a_last - a_cs)

    stack = jnp.concatenate([dtp, e_in, e_out], axis=0)
    ex = jnp.dot(jnp.concatenate(_split3(stack), axis=1), e_ref[...], preferred_element_type=F32)
    dt_e, ein_e, eout_e = ex[0:t], ex[t:2 * t], ex[2 * t:3 * t]

    xdt = xs * dt_e
    xdt_b = xdt.astype(BF16)
    xw_b = (xdt * eout_e).astype(BF16)

    pad = jnp.zeros((LANES - t, LANES), F32)
    a_cs_t = jnp.concatenate([a_cs, pad], axis=0).T[:, 0:t]
    li = lax.broadcasted_iota(jnp.int32, (t, t), 0)
    si = lax.broadcasted_iota(jnp.int32, (t, t), 1)
    causal = li >= si

    z = z_ref[...]
    nt = (((1,), (1,)), ((), ()))
    tn = (((0,), (0,)), ((), ()))
    for g in range(N_GROUPS):
        gs = slice(g * GROUP_W, (g + 1) * GROUP_W)
        ns = slice(g * D_STATE, (g + 1) * D_STATE)
        c_g = cm[:, ns].astype(BF16)
        b_g = bm[:, ns].astype(BF16)
        cb = lax.dot_general(c_g, b_g, nt, preferred_element_type=F32)
        h_prev = h_ref[:, gs]
        y_off = jnp.dot(c_g, h_prev.astype(BF16), preferred_element_type=F32) * ein_e[:, gs]
        y_heads = []
        for r in range(HEADS_PER_GROUP):
            h = g * HEADS_PER_GROUP + r
            seg = a_cs[:, h:h + 1] - a_cs_t[h:h + 1, :]
            m_h = (cb * jnp.where(causal, jnp.exp(seg), 0.0)).astype(BF16)
            y_heads.append(jnp.dot(m_h, xdt_b[:, h * HEADDIM:(h + 1) * HEADDIM], preferred_element_type=F32))
        y = jnp.concatenate(y_heads, axis=1) + y_off
        states = lax.dot_general(b_g, xw_b[:, gs], tn, preferred_element_type=F32)
        h_ref[:, gs] = h_prev * ein_e[t - 1:t, gs] + states

        y = y + dexp_ref[:, gs] * xs[:, gs]
        v = y * _silu(z[:, gs])
        y_ref[:, gs] = _rms(v, ng_ref[:, gs]).astype(y_ref.dtype)

    @pl.when(pl.program_id(1) == pl.num_programs(1) - 1)
    def _():
        for i in range(D_INNER // LANES):
            cs = slice(i * LANES, (i + 1) * LANES)
            hout_ref[0, cs, :] = h_ref[:, cs].T


def _ssd(proj, pdt, row0, nb, seq, conv_w, conv_b, dt_bias, a_log, d_exp, norm_g, expand, prev, h0):
    t = min(CHUNK, seq)
    nc = seq // t
    rb = row0 // t
    xi, bi, ci = D_INNER // D_INNER, (2 * D_INNER) // BC_W, (2 * D_INNER + BC_W) // BC_W
    rows = lambda b, c: rb + b * nc + c
    full = lambda shape: pl.BlockSpec(shape, lambda b, c: (0, 0))
    in_specs = [
        pl.BlockSpec((t, D_INNER), lambda b, c: (rows(b, c), 0)),
        pl.BlockSpec((t, D_INNER), lambda b, c: (rows(b, c), xi)),
        pl.BlockSpec((t, BC_W), lambda b, c: (rows(b, c), bi)),
        pl.BlockSpec((t, BC_W), lambda b, c: (rows(b, c), ci)),
        pl.BlockSpec((t, LANES), lambda b, c: (rows(b, c), 0)),
        pl.BlockSpec((SSD_K, D_INNER), lambda b, c: (0, 0)),
        pl.BlockSpec((SSD_K, BC_W), lambda b, c: (0, D_INNER // BC_W)),
        pl.BlockSpec((SSD_K, BC_W), lambda b, c: (0, D_INNER // BC_W + 1)),
        pl.BlockSpec((1, D_INNER), lambda b, c: (0, 0)),
        pl.BlockSpec((1, BC_W), lambda b, c: (0, D_INNER // BC_W)),
        pl.BlockSpec((1, BC_W), lambda b, c: (0, D_INNER // BC_W + 1)),
        full((1, LANES)), full((1, LANES)), full((1, D_INNER)), full((1, D_INNER)), full((3 * LANES, D_INNER)),
        pl.BlockSpec((1, SSD_HIST, D_INNER), lambda b, c: (b, 0, 0)),
        pl.BlockSpec((1, SSD_HIST, BC_W), lambda b, c: (b, 0, D_INNER // BC_W)),
        pl.BlockSpec((1, SSD_HIST, BC_W), lambda b, c: (b, 0, D_INNER // BC_W + 1)),
        pl.BlockSpec((1, D_INNER, D_STATE), lambda b, c: (b, 0, 0)),
    ]
    return pl.pallas_call(
        functools.partial(_ssd_body, t=t),
        grid=(nb, nc),
        in_specs=in_specs,
        out_specs=[
            pl.BlockSpec((t, D_INNER), lambda b, c: (b * nc + c, 0)),
            pl.BlockSpec((1, D_INNER, D_STATE), lambda b, c: (b, 0, 0)),
        ],
        out_shape=[
            jax.ShapeDtypeStruct((nb * seq, D_INNER), BF16),
            jax.ShapeDtypeStruct((nb, D_INNER, D_STATE), F32),
        ],
        scratch_shapes=[
            pltpu.VMEM((SSD_HIST + t, D_INNER), F32),
            pltpu.VMEM((SSD_HIST + t, BC_W), F32),
            pltpu.VMEM((SSD_HIST + t, BC_W), F32),
            pltpu.VMEM((D_STATE, D_INNER), F32),
        ],
        compiler_params=_params(("parallel", "arbitrary")),
        name="ssd",
    )(proj, proj, proj, proj, pdt, conv_w, conv_w, conv_w, conv_b, conv_b, conv_b,
      dt_bias, a_log, d_exp, norm_g, expand, prev, prev, prev, h0)


CM_ROWS = 64


def _cm_body(v_ref, g_ref, w_ref, b_ref, lg_ref, lb_ref, p_ref, o_ref, s_ref, buf, ybuf, *, t):
    @pl.when(pl.program_id(1) == 0)
    def _():
        buf[0:CM_HIST, :] = p_ref[0]

    buf[CM_HIST:CM_HIST + t, :] = v_ref[...] * _sigmoid(g_ref[...])
    base = CM_HIST - (CM_K - 1)
    rt = min(CM_ROWS, t)
    win_rows = rt + CM_HIST

    def cols(ci, carry):
        cs = pl.ds(pl.multiple_of(ci * LANES, LANES), LANES)
        for r0 in range(0, t, rt):
            win = buf[r0:r0 + win_rows, cs]
            acc = b_ref[:, cs] + w_ref[CM_K - 1:CM_K, cs] * buf[r0 + CM_HIST:r0 + CM_HIST + rt, cs]
            for s in range(SUBLANES):
                sh = win if s == 0 else pltpu.roll(win, win_rows - s, 0)
                for a in range(CM_HIST // SUBLANES):
                    k = SUBLANES * a + s - base
                    if 0 <= k < CM_K - 1:
                        acc = acc + w_ref[k:k + 1, cs] * sh[SUBLANES * a:SUBLANES * a + rt, :]
            ybuf[r0:r0 + rt, cs] = acc
        return carry

    lax.fori_loop(0, D_MODEL // LANES, cols, 0)

    hist = buf[t:t + CM_HIST, :]
    buf[0:CM_HIST, :] = hist

    @pl.when(pl.program_id(1) == pl.num_programs(1) - 1)
    def _():
        s_ref[0] = hist

    y = ybuf[...]
    mu = jnp.mean(y, axis=-1, keepdims=True)
    yc = y - mu
    var = jnp.mean(yc * yc, axis=-1, keepdims=True)
    o_ref[...] = _silu(yc * lax.rsqrt(var + EPS) * lg_ref[...] + lb_ref[...]).astype(o_ref.dtype)


def _convmod(proj, row0, nb, seq, dw_w, dw_b, ln_g, ln_b, prev):
    t = min(256, seq)
    nc = seq // t
    rb = row0 // t
    vi = (2 * D_INNER + 2 * BC_W) // D_MODEL
    rows = lambda b, c: rb + b * nc + c
    full = lambda shape: pl.BlockSpec(shape, lambda b, c: (0, 0))
    return pl.pallas_call(
        functools.partial(_cm_body, t=t),
        grid=(nb, nc),
        in_specs=[
            pl.BlockSpec((t, D_MODEL), lambda b, c: (rows(b, c), vi)),
            pl.BlockSpec((t, D_MODEL), lambda b, c: (rows(b, c), vi + 1)),
            full((CM_K, D_MODEL)), full((1, D_MODEL)), full((1, D_MODEL)), full((1, D_MODEL)),
            pl.BlockSpec((1, CM_HIST, D_MODEL), lambda b, c: (b, 0, 0)),
        ],
        out_specs=[
            pl.BlockSpec((t, D_MODEL), lambda b, c: (b * nc + c, 0)),
            pl.BlockSpec((1, CM_HIST, D_MODEL), lambda b, c: (b, 0, 0)),
        ],
        out_shape=[
            jax.ShapeDtypeStruct((nb * seq, D_MODEL), BF16),
            jax.ShapeDtypeStruct((nb, CM_HIST, D_MODEL), F32),
        ],
        scratch_shapes=[pltpu.VMEM((CM_HIST + t, D_MODEL), F32), pltpu.VMEM((t, D_MODEL), F32)],
        compiler_params=_params(("parallel", "arbitrary")),
        name="convmod",
    )(proj, proj, dw_w, dw_b, ln_g, ln_b, prev)


def _attn_body(q_ref, k_ref, v_ref, o_ref, kb, vb, *, per_head):
    @pl.when(pl.program_id(1) == 0)
    def _():
        if per_head:
            for h in range(XA_HEADS):
                hs = slice(h * XA_DIM, (h + 1) * XA_DIM)
                kb[:, hs] = k_ref[0, :, h, :].astype(BF16)
                vb[:, hs] = v_ref[0, :, h, :].astype(BF16)
        else:
            kb[...] = k_ref[0].astype(BF16)
            vb[...] = v_ref[0].astype(BF16)

    nt = (((1,), (1,)), ((), ()))
    for h in range(XA_HEADS):
        hs = slice(h * XA_DIM, (h + 1) * XA_DIM)
        s = lax.dot_general(q_ref[:, hs], kb[:, hs], nt, preferred_element_type=F32) * (XA_DIM ** -0.5)
        e = jnp.exp(s - jnp.max(s, axis=-1, keepdims=True))
        p = (e / jnp.sum(e, axis=-1, keepdims=True)).astype(BF16)
        o_ref[:, hs] = jnp.dot(p, vb[:, hs], preferred_element_type=F32).astype(o_ref.dtype)


def _attend(q, row0, nb, seq, k, v):
    tq = min(512, seq)
    nq = seq // tq
    rb = row0 // tq
    per_head = k.ndim == 4
    if per_head:
        kv_spec = pl.BlockSpec((1, N_MEM, XA_HEADS, XA_DIM), lambda b, i: (b, 0, 0, 0))
    else:
        kv_spec = pl.BlockSpec((1, N_MEM, D_MODEL), lambda b, i: (b, 0, 0))
    return pl.pallas_call(
        functools.partial(_attn_body, per_head=per_head),
        grid=(nb, nq),
        in_specs=[pl.BlockSpec((tq, D_MODEL), lambda b, i: (rb + b * nq + i, 0)), kv_spec, kv_spec],
        out_specs=pl.BlockSpec((tq, D_MODEL), lambda b, i: (b * nq + i, 0)),
        out_shape=jax.ShapeDtypeStruct((nb * seq, D_MODEL), BF16),
        scratch_shapes=[pltpu.VMEM((N_MEM, D_MODEL), BF16), pltpu.VMEM((N_MEM, D_MODEL), BF16)],
        compiler_params=_params(("parallel", "arbitrary")),
        name="attend",
    )(q, k, v)


def _pad_rows_front(a, rows):
    return jnp.pad(a, ((0, 0), (rows - a.shape[1], 0), (0, 0)))


def kernel(x_prompt, x_sample, mem_prompt, cache_ssd_conv, cache_ssm_state, cache_cm_conv, cache_mem_k, cache_mem_v, ffn1_norm, ffn1_wi, ffn1_wo, mix_norm, w_in, ssd_conv_w, ssd_conv_b, ssd_dt_bias, ssd_a_log, ssd_d, ssd_norm, ssd_w_out, cm_dw_w, cm_dw_b, cm_ln_g, cm_ln_b, cm_w_out, w_mix_out, xa_norm, mem_norm, xa_wq, xa_wk, xa_wv, xa_wo, ffn2_norm, ffn2_wi, ffn2_wo, final_norm):
    assert ffn1_wi.shape[0] == 1, "single layer"
    bp, lp, _ = x_prompt.shape
    bs, ls, _ = x_sample.shape
    np_rows, ns_rows = bp * lp, bs * ls

    i1 = D_INNER
    i2 = i1 + XBC
    i3 = i2 + N_HEADS
    w = w_in[0]
    w_main = jnp.concatenate([w[:, :i2], w[:, i3:]], axis=1).astype(BF16)
    w_dt = jnp.pad(w[:, i2:i3], ((0, 0), (0, LANES - N_HEADS))).astype(BF16)
    cast = lambda a: a[0].astype(BF16)

    x1 = _ffn(x_prompt.reshape(np_rows, D_MODEL), x_sample.reshape(ns_rows, D_MODEL), ffn1_norm[0],
              cast(ffn1_wi), cast(ffn1_wo))
    proj = _norm_mm(x1, mix_norm[0], w_main, bm=1040, bn=1024, name="in_proj")
    pdt = _norm_mm(x1, mix_norm[0], w_dt, bm=1040, bn=LANES, name="dt_proj")

    pad_h = lambda a: jnp.pad(a[0].reshape(1, N_HEADS), ((0, 0), (0, LANES - N_HEADS)))
    d_exp = jnp.repeat(ssd_d[0], HEADDIM).reshape(1, D_INNER)
    expand = (lax.broadcasted_iota(jnp.int32, (LANES, D_INNER), 1) // HEADDIM
              == lax.broadcasted_iota(jnp.int32, (LANES, D_INNER), 0)).astype(BF16)
    ssd_args = (ssd_conv_w[0], ssd_conv_b[0].reshape(1, XBC), pad_h(ssd_dt_bias), pad_h(ssd_a_log), d_exp,
                ssd_norm[0].reshape(1, D_INNER), jnp.concatenate([expand] * 3, axis=0))
    yn_p, hp = _ssd(proj, pdt, 0, bp, lp, *ssd_args,
                    jnp.zeros((bp, SSD_HIST, XBC), F32), jnp.zeros((bp, D_INNER, D_STATE), F32))
    yn_s, hs = _ssd(proj, pdt, np_rows, bs, ls, *ssd_args,
                    _pad_rows_front(cache_ssd_conv[0], SSD_HIST), cache_ssm_state[0].reshape(bs, D_INNER, D_STATE))
    xbc_cols = slice(D_INNER, D_INNER + XBC)
    p_ssd_conv = jnp.stack([proj[(b + 1) * lp - (SSD_K - 1):(b + 1) * lp, xbc_cols] for b in range(bp)])
    s_ssd_conv = jnp.concatenate(
        [cache_ssd_conv[0], proj[np_rows:, xbc_cols].reshape(bs, ls, XBC)], axis=1)[:, -(SSD_K - 1):]

    cm_args = (cm_dw_w[0], cm_dw_b[0].reshape(1, D_MODEL), cm_ln_g[0].reshape(1, D_MODEL),
               cm_ln_b[0].reshape(1, D_MODEL))
    un_p, cp = _convmod(proj, 0, bp, lp, *cm_args, jnp.zeros((bp, CM_HIST, D_MODEL), F32))
    un_s, cs = _convmod(proj, np_rows, bs, ls, *cm_args, _pad_rows_front(cache_cm_conv[0], CM_HIST))

    m = _merge(yn_p, yn_s, un_p, un_s, cast(ssd_w_out), cast(cm_w_out), proj)
    x2 = _mm_res(m, None, cast(w_mix_out), x1, bm=1040, bn=1024, name="mix_out")

    mem = mem_prompt.reshape(bp * N_MEM, D_MODEL)
    k_p = _norm_mm(mem, mem_norm[0], cast(xa_wk), bm=bp * N_MEM, bn=1024, name="mem_k")
    v_p = _norm_mm(mem, mem_norm[0], cast(xa_wv), bm=bp * N_MEM, bn=1024, name="mem_v")
    q = _norm_mm(x2, xa_norm[0], cast(xa_wq), bm=1040, bn=1024, out_dtype=BF16, name="q_proj")
    o_p = _attend(q, 0, bp, lp, k_p.reshape(bp, N_MEM, D_MODEL), v_p.reshape(bp, N_MEM, D_MODEL))
    o_s = _attend(q, np_rows, bs, ls, cache_mem_k[0], cache_mem_v[0])
    x3 = _mm_res(o_p, o_s, cast(xa_wo), x2, bm=640, bn=1024, name="attn_out")

    y_p, y_s = _ffn(x3, None, ffn2_norm[0], cast(ffn2_wi), cast(ffn2_wo), final_g=final_norm, split_rows=np_rows)

    kv_shape = (1, bp, N_MEM, XA_HEADS, XA_DIM)
    state_shape = lambda b: (1, b, N_HEADS, HEADDIM, D_STATE)
    return (
        y_p.reshape(bp, lp, D_MODEL),
        y_s.reshape(bs, ls, D_MODEL),
        p_ssd_conv[None],
        hp.reshape(state_shape(bp)),
        cp[:, CM_HIST - (CM_K - 1):][None],
        k_p.reshape(kv_shape),
        v_p.reshape(kv_shape),
        s_ssd_conv[None],
        hs.reshape(state_shape(bs)),
        cs[:, CM_HIST - (CM_K - 1):][None],
    )
```

```python
import functools

import jax
import jax.numpy as jnp
from jax import lax
from jax.experimental import pallas as pl
from jax.experimental.pallas import tpu as pltpu

F32 = jnp.float32
BF16 = jnp.bfloat16

D_MODEL = 2048
D_FF = 4 * D_MODEL
D_INNER = 2 * D_MODEL
HEADDIM = 64
N_HEADS = D_INNER // HEADDIM
N_GROUPS = 8
HEADS_PER_GROUP = N_HEADS // N_GROUPS
D_STATE = 128
GROUP_W = D_INNER // N_GROUPS
BC_W = N_GROUPS * D_STATE
SSD_K = 4
XBC = D_INNER + 2 * BC_W
CM_K = 31
CM_HIST = 32
SSD_HIST = 8
N_MEM = 256
XA_HEADS = 4
XA_DIM = D_MODEL // XA_HEADS
CHUNK = 64
EPS = 1e-6
LANES = 128
SUBLANES = 8
SSD_COLS = 2 * D_INNER + 2 * BC_W
VMEM_LIMIT = 56 * 1024 * 1024


def _params(sem):
    return pltpu.CompilerParams(dimension_semantics=sem, vmem_limit_bytes=VMEM_LIMIT)


def _rms(x, g):
    return x * lax.rsqrt(jnp.mean(x * x, axis=-1, keepdims=True) + EPS) * g


def _sigmoid(x):
    return 0.5 * jnp.tanh(0.5 * x) + 0.5


def _silu(x):
    return x * _sigmoid(x)


def _on_row_tiles(i, rows_main, full_fn, split_fn):
    last = pl.num_programs(0) - 1
    pl.when(i != last)(full_fn)
    pl.when(i == last)(lambda: split_fn(rows_main))


def _ffn_body(*refs, final, tail, split, r):
    it = iter(refs)
    x_ref = next(it)
    xt_ref = next(it) if tail else None
    g_ref, wa_ref, wb_ref, wo_ref = next(it), next(it), next(it), next(it)
    fg_ref = next(it) if final else None
    o_ref = next(it)
    ot_ref = next(it) if split else None
    h_ref, acc_ref = next(it), next(it)
    i, j = pl.program_id(0), pl.program_id(1)

    def norm_in(x):
        return _rms(x, g_ref[...]).astype(BF16)

    def finish(x, acc):
        y = x + 0.5 * acc
        return _rms(y, fg_ref[...]) if final else y

    @pl.when(j == 0)
    def _():
        acc_ref[...] = jnp.zeros_like(acc_ref)
        if tail:
            def full():
                h_ref[...] = norm_in(x_ref[...])

            def last(rm):
                h_ref[0:rm, :] = norm_in(x_ref[0:rm, :])
                h_ref[rm:, :] = norm_in(xt_ref[...])

            _on_row_tiles(i, r, full, last)
        else:
            h_ref[...] = norm_in(x_ref[...])

    h = h_ref[...]
    a = jnp.dot(h, wa_ref[...], preferred_element_type=F32)
    b = jnp.dot(h, wb_ref[...], preferred_element_type=F32)
    acc_ref[...] += jnp.dot((_silu(a) * b).astype(BF16), wo_ref[...], preferred_element_type=F32)

    @pl.when(j == pl.num_programs(1) - 1)
    def _():
        if tail or split:
            def full():
                o_ref[...] = finish(x_ref[...], acc_ref[...])

            def last(rm):
                o_ref[0:rm, :] = finish(x_ref[0:rm, :], acc_ref[0:rm, :])
                yt = finish(xt_ref[...] if tail else x_ref[rm:, :], acc_ref[rm:, :])
                if split:
                    ot_ref[...] = yt
                else:
                    o_ref[rm:, :] = yt

            _on_row_tiles(i, r, full, last)
        else:
            o_ref[...] = finish(x_ref[...], acc_ref[...])


def _ffn(x, x_tail, g, wi, wo, final_g=None, split_rows=None, *, bm=640, bf=512):
    d = x.shape[1]
    tail, split, final = x_tail is not None, split_rows is not None, final_g is not None
    m = x.shape[0] + (x_tail.shape[0] if tail else 0)
    n_main = x.shape[0] if tail else (split_rows if split else m)
    nm, nf = m // bm, D_FF // bf
    r = n_main - (nm - 1) * bm
    assert nm * bm == m and 0 < r <= bm and (r == bm or m - n_main == bm - r)
    row = lambda i, j: (i, 0)
    const = lambda i, j: (0, 0)
    in_specs = [pl.BlockSpec((bm, d), row)]
    args = [x]
    if tail:
        in_specs.append(pl.BlockSpec((bm - r, d), const))
        args.append(x_tail)
    in_specs += [
        pl.BlockSpec((1, d), const),
        pl.BlockSpec((d, bf), lambda i, j: (0, j)),
        pl.BlockSpec((d, bf), lambda i, j: (0, j + nf)),
        pl.BlockSpec((bf, d), lambda i, j: (j, 0)),
    ]
    args += [g.reshape(1, d), wi, wi, wo]
    if final:
        in_specs.append(pl.BlockSpec((1, d), const))
        args.append(final_g.reshape(1, d))
    if split:
        out_specs = [pl.BlockSpec((bm, d), row), pl.BlockSpec((bm - r, d), const)]
        out_shape = [jax.ShapeDtypeStruct((n_main, d), F32), jax.ShapeDtypeStruct((m - n_main, d), F32)]
    else:
        out_specs = pl.BlockSpec((bm, d), row)
        out_shape = jax.ShapeDtypeStruct((m, d), F32)
    return pl.pallas_call(
        functools.partial(_ffn_body, final=final, tail=tail, split=split, r=r),
        grid=(nm, nf),
        in_specs=in_specs,
        out_specs=out_specs,
        out_shape=out_shape,
        scratch_shapes=[pltpu.VMEM((bm, d), BF16), pltpu.VMEM((bm, d), F32)],
        compiler_params=_params(("arbitrary", "arbitrary")),
        name="ffn",
    )(*args)


def _norm_mm_body(x_ref, g_ref, w_ref, o_ref, h_ref):
    @pl.when(pl.program_id(1) == 0)
    def _():
        h_ref[...] = _rms(x_ref[...], g_ref[...]).astype(BF16)

    o_ref[...] = jnp.dot(h_ref[...], w_ref[...].astype(BF16), preferred_element_type=F32).astype(o_ref.dtype)


def _norm_mm(x, g, w, *, bm, bn, n=None, out_dtype=F32, name="norm_mm"):
    m, d = x.shape
    n = w.shape[1] if n is None else n
    return pl.pallas_call(
        _norm_mm_body,
        grid=(m // bm, n // bn),
        in_specs=[
            pl.BlockSpec((bm, d), lambda i, j: (i, 0)),
            pl.BlockSpec((1, d), lambda i, j: (0, 0)),
            pl.BlockSpec((d, bn), lambda i, j: (0, j)),
        ],
        out_specs=pl.BlockSpec((bm, bn), lambda i, j: (i, j)),
        out_shape=jax.ShapeDtypeStruct((m, n), out_dtype),
        scratch_shapes=[pltpu.VMEM((bm, d), BF16)],
        compiler_params=_params(("parallel", "arbitrary")),
        name=name,
    )(x, g.reshape(1, d), w)


def _mm_res_body(a_ref, *rest, tail, r):
    at_ref = rest[0] if tail else None
    w_ref, r_ref, o_ref = rest[-3:]

    def tile(a, res):
        return res + jnp.dot(a, w_ref[...].astype(BF16), preferred_element_type=F32)

    if tail:
        def full():
            o_ref[...] = tile(a_ref[...], r_ref[...])

        def last(rm):
            o_ref[0:rm, :] = tile(a_ref[0:rm, :], r_ref[0:rm, :])
            o_ref[rm:, :] = tile(at_ref[...], r_ref[rm:, :])

        _on_row_tiles(pl.program_id(0), r, full, last)
    else:
        o_ref[...] = tile(a_ref[...], r_ref[...])


def _mm_res(a, a_tail, w, res, *, bm, bn, name):
    k = a.shape[1]
    m, n = res.shape
    tail = a_tail is not None
    nm = m // bm
    r = a.shape[0] - (nm - 1) * bm
    assert nm * bm == m and (not tail or a_tail.shape[0] == bm - r)
    in_specs = [pl.BlockSpec((bm, k), lambda i, j: (i, 0))]
    args = [a]
    if tail:
        in_specs.append(pl.BlockSpec((bm - r, k), lambda i, j: (0, 0)))
        args.append(a_tail)
    in_specs += [pl.BlockSpec((k, bn), lambda i, j: (0, j)), pl.BlockSpec((bm, bn), lambda i, j: (i, j))]
    return pl.pallas_call(
        functools.partial(_mm_res_body, tail=tail, r=r),
        grid=(nm, n // bn),
        in_specs=in_specs,
        out_specs=pl.BlockSpec((bm, bn), lambda i, j: (i, j)),
        out_shape=jax.ShapeDtypeStruct((m, n), F32),
        compiler_params=_params(("parallel", "arbitrary")),
        name=name,
    )(*args, w, res)


def _merge_body(y_ref, yt_ref, u_ref, ut_ref, ws_ref, wc_ref, gs_ref, gc_ref, o_ref, *, r):
    def tile(y, u, gs, gc):
        ssd = jnp.dot(y, ws_ref[...], preferred_element_type=F32)
        cm = jnp.dot(u, wc_ref[...], preferred_element_type=F32)
        return (_sigmoid(gs) * ssd + _sigmoid(gc) * cm).astype(o_ref.dtype)

    def full():
        o_ref[...] = tile(y_ref[...], u_ref[...], gs_ref[...], gc_ref[...])

    def last(rm):
        o_ref[0:rm, :] = tile(y_ref[0:rm, :], u_ref[0:rm, :], gs_ref[0:rm, :], gc_ref[0:rm, :])
        o_ref[rm:, :] = tile(yt_ref[...], ut_ref[...], gs_ref[rm:, :], gc_ref[rm:, :])

    _on_row_tiles(pl.program_id(0), r, full, last)


def _merge(yn, yn_tail, un, un_tail, w_ssd, w_cm, proj, *, bm=640, bn=512):
    m = proj.shape[0]
    nm = m // bm
    r = yn.shape[0] - (nm - 1) * bm
    assert nm * bm == m and yn_tail.shape[0] == bm - r
    gs_off = (2 * D_MODEL) // bn
    gc_off = (3 * D_MODEL) // bn
    return pl.pallas_call(
        functools.partial(_merge_body, r=r),
        grid=(nm, D_MODEL // bn),
        in_specs=[
            pl.BlockSpec((bm, D_INNER), lambda i, j: (i, 0)),
            pl.BlockSpec((bm - r, D_INNER), lambda i, j: (0, 0)),
            pl.BlockSpec((bm, D_MODEL), lambda i, j: (i, 0)),
            pl.BlockSpec((bm - r, D_MODEL), lambda i, j: (0, 0)),
            pl.BlockSpec((D_INNER, bn), lambda i, j: (0, j)),
            pl.BlockSpec((D_MODEL, bn), lambda i, j: (0, j)),
            pl.BlockSpec((bm, bn), lambda i, j: (i, j + gs_off)),
            pl.BlockSpec((bm, bn), lambda i, j: (i, j + gc_off)),
        ],
        out_specs=pl.BlockSpec((bm, bn), lambda i, j: (i, j)),
        out_shape=jax.ShapeDtypeStruct((m, D_MODEL), BF16),
        compiler_params=_params(("parallel", "arbitrary")),
        name="merge",
    )(yn, yn_tail, un, un_tail, w_ssd, w_cm, proj, proj)


def _split3(v):
    hi = v.astype(BF16)
    r = v - hi.astype(F32)
    mid = r.astype(BF16)
    lo = (r - mid.astype(F32)).astype(BF16)
    return hi, mid, lo


def _ssd_body(z_ref, xs_ref, b_ref, c_ref, dt_ref, wx_ref, wb_ref, wc_ref, bx_ref, bb_ref, bc_ref,
              dtb_ref, alog_ref, dexp_ref, ng_ref, e_ref, px_ref, pb_ref, pc_ref, h0_ref,
              y_ref, hout_ref, xhist, bhist, chist, h_ref, *, t):
    @pl.when(pl.program_id(1) == 0)
    def _():
        xhist[...] = px_ref[0]
        bhist[...] = pb_ref[0]
        chist[...] = pc_ref[0]
        for i in range(D_INNER // LANES):
            cs = slice(i * LANES, (i + 1) * LANES)
            h_ref[:, cs] = h0_ref[0, cs, :].T

    def conv_silu(hist, raw_ref, w_ref, bias_ref, cs):
        raw = raw_ref[:, cs]
        win = jnp.concatenate([hist[:, cs], raw], axis=0)
        acc = bias_ref[:, cs] + w_ref[SSD_K - 1:SSD_K, cs] * raw
        for k in range(SSD_K - 1):
            back = SSD_K - 1 - k
            acc = acc + w_ref[k:k + 1, cs] * pltpu.roll(win, t + back, 0)[0:t, :]
        hist[:, cs] = raw[t - SSD_HIST:t, :]
        return _silu(acc)

    every = slice(None)
    bm = conv_silu(bhist, b_ref, wb_ref, bb_ref, every)
    cm = conv_silu(chist, c_ref, wc_ref, bc_ref, every)

    dtp = jax.nn.softplus(dt_ref[...] + dtb_ref[...])
    d_a = dtp * (-jnp.exp(alog_ref[...]))
    row = lax.broadcasted_iota(jnp.int32, (t, LANES), 0)
    a_cs = d_a
    s = 1
    while s < t:
        a_cs = a_cs + jnp.where(row >= s, pltpu.roll(a_cs, s, 0), 0.0)
        s *= 2
    a_last = a_cs[t - 1:t, :]
    e_in = jnp.exp(a_cs)
    e_out = jnp.exp(a_last - a_cs)
    factors = jnp.concatenate(_split3(jnp.concatenate([dtp, e_in, e_out], axis=0)), axis=1)

    pad = jnp.zeros((LANES - t, LANES), F32)
    a_cs_t = jnp.concatenate([a_cs, pad], axis=0).T[:, 0:t]
    li = lax.broadcasted_iota(jnp.int32, (t, t), 0)
    si = lax.broadcasted_iota(jnp.int32, (t, t), 1)
    causal = li >= si

    nt = (((1,), (1,)), ((), ()))
    tn = (((0,), (0,)), ((), ()))
    for g in range(N_GROUPS):
        gs = slice(g * GROUP_W, (g + 1) * GROUP_W)
        ns = slice(g * D_STATE, (g + 1) * D_STATE)
        xs = conv_silu(xhist, xs_ref, wx_ref, bx_ref, gs)
        ex = jnp.dot(factors, e_ref[:, gs], preferred_element_type=F32)
        dt_e, ein_e, eout_e = ex[0:t], ex[t:2 * t], ex[2 * t:3 * t]
        xdt = xs * dt_e
        xdt_b = xdt.astype(BF16)
        xw_b = (xdt * eout_e).astype(BF16)

        c_g = cm[:, ns].astype(BF16)
        b_g = bm[:, ns].astype(BF16)
        cb = lax.dot_general(c_g, b_g, nt, preferred_element_type=F32)
        h_prev = h_ref[:, gs]
        y_off = jnp.dot(c_g, h_prev.astype(BF16), preferred_element_type=F32) * ein_e
        y_heads = []
        for r in range(HEADS_PER_GROUP):
            h = g * HEADS_PER_GROUP + r
            seg = a_cs[:, h:h + 1] - a_cs_t[h:h + 1, :]
            m_h = (cb * jnp.where(causal, jnp.exp(seg), 0.0)).astype(BF16)
            y_heads.append(jnp.dot(m_h, xdt_b[:, r * HEADDIM:(r + 1) * HEADDIM], preferred_element_type=F32))
        y = jnp.concatenate(y_heads, axis=1) + y_off
        states = lax.dot_general(b_g, xw_b, tn, preferred_element_type=F32)
        h_ref[:, gs] = h_prev * ein_e[t - 1:t, :] + states

        y = y + dexp_ref[:, gs] * xs
        v = y * _silu(z_ref[:, gs])
        y_ref[:, gs] = _rms(v, ng_ref[:, gs]).astype(y_ref.dtype)

    @pl.when(pl.program_id(1) == pl.num_programs(1) - 1)
    def _():
        for i in range(D_INNER // LANES):
            cs = slice(i * LANES, (i + 1) * LANES)
            hout_ref[0, cs, :] = h_ref[:, cs].T


def _ssd(proj, pdt, row0, nb, seq, conv_w, conv_b, dt_bias, a_log, d_exp, norm_g, expand, prev, h0):
    t = min(CHUNK, seq)
    nc = seq // t
    rb = row0 // t
    xi, bi, ci = D_INNER // D_INNER, (2 * D_INNER) // BC_W, (2 * D_INNER + BC_W) // BC_W
    rows = lambda b, c: rb + b * nc + c
    full = lambda shape: pl.BlockSpec(shape, lambda b, c: (0, 0))
    in_specs = [
        pl.BlockSpec((t, D_INNER), lambda b, c: (rows(b, c), 0)),
        pl.BlockSpec((t, D_INNER), lambda b, c: (rows(b, c), xi)),
        pl.BlockSpec((t, BC_W), lambda b, c: (rows(b, c), bi)),
        pl.BlockSpec((t, BC_W), lambda b, c: (rows(b, c), ci)),
        pl.BlockSpec((t, LANES), lambda b, c: (rows(b, c), 0)),
        pl.BlockSpec((SSD_K, D_INNER), lambda b, c: (0, 0)),
        pl.BlockSpec((SSD_K, BC_W), lambda b, c: (0, D_INNER // BC_W)),
        pl.BlockSpec((SSD_K, BC_W), lambda b, c: (0, D_INNER // BC_W + 1)),
        pl.BlockSpec((1, D_INNER), lambda b, c: (0, 0)),
        pl.BlockSpec((1, BC_W), lambda b, c: (0, D_INNER // BC_W)),
        pl.BlockSpec((1, BC_W), lambda b, c: (0, D_INNER // BC_W + 1)),
        full((1, LANES)), full((1, LANES)), full((1, D_INNER)), full((1, D_INNER)), full((3 * LANES, D_INNER)),
        pl.BlockSpec((1, SSD_HIST, D_INNER), lambda b, c: (b, 0, 0)),
        pl.BlockSpec((1, SSD_HIST, BC_W), lambda b, c: (b, 0, D_INNER // BC_W)),
        pl.BlockSpec((1, SSD_HIST, BC_W), lambda b, c: (b, 0, D_INNER // BC_W + 1)),
        pl.BlockSpec((1, D_INNER, D_STATE), lambda b, c: (b, 0, 0)),
    ]
    return pl.pallas_call(
        functools.partial(_ssd_body, t=t),
        grid=(nb, nc),
        in_specs=in_specs,
        out_specs=[
            pl.BlockSpec((t, D_INNER), lambda b, c: (b * nc + c, 0)),
            pl.BlockSpec((1, D_INNER, D_STATE), lambda b, c: (b, 0, 0)),
        ],
        out_shape=[
            jax.ShapeDtypeStruct((nb * seq, D_INNER), BF16),
            jax.ShapeDtypeStruct((nb, D_INNER, D_STATE), F32),
        ],
        scratch_shapes=[
            pltpu.VMEM((SSD_HIST, D_INNER), F32),
            pltpu.VMEM((SSD_HIST, BC_W), F32),
            pltpu.VMEM((SSD_HIST, BC_W), F32),
            pltpu.VMEM((D_STATE, D_INNER), F32),
        ],
        compiler_params=_params(("parallel", "arbitrary")),
        name="ssd",
    )(proj, proj, proj, proj, pdt, conv_w, conv_w, conv_w, conv_b, conv_b, conv_b,
      dt_bias, a_log, d_exp, norm_g, expand, prev, prev, prev, h0)


CM_ROWS = 64


def _cm_body(v_ref, g_ref, w_ref, b_ref, lg_ref, lb_ref, p_ref, o_ref, s_ref, buf, ybuf, *, t):
    @pl.when(pl.program_id(1) == 0)
    def _():
        buf[0:CM_HIST, :] = p_ref[0]

    buf[CM_HIST:CM_HIST + t, :] = v_ref[...] * _sigmoid(g_ref[...])
    base = CM_HIST - (CM_K - 1)
    rt = min(CM_ROWS, t)
    win_rows = rt + CM_HIST

    def cols(ci, carry):
        cs = pl.ds(pl.multiple_of(ci * LANES, LANES), LANES)
        for r0 in range(0, t, rt):
            win = buf[r0:r0 + win_rows, cs]
            acc = b_ref[:, cs] + w_ref[CM_K - 1:CM_K, cs] * buf[r0 + CM_HIST:r0 + CM_HIST + rt, cs]
            for s in range(SUBLANES):
                sh = win if s == 0 else pltpu.roll(win, win_rows - s, 0)
                for a in range(CM_HIST // SUBLANES):
                    k = SUBLANES * a + s - base
                    if 0 <= k < CM_K - 1:
                        acc = acc + w_ref[k:k + 1, cs] * sh[SUBLANES * a:SUBLANES * a + rt, :]
            ybuf[r0:r0 + rt, cs] = acc
        return carry

    lax.fori_loop(0, D_MODEL // LANES, cols, 0)

    hist = buf[t:t + CM_HIST, :]
    buf[0:CM_HIST, :] = hist

    @pl.when(pl.program_id(1) == pl.num_programs(1) - 1)
    def _():
        s_ref[0] = hist

    y = ybuf[...]
    mu = jnp.mean(y, axis=-1, keepdims=True)
    yc = y - mu
    var = jnp.mean(yc * yc, axis=-1, keepdims=True)
    o_ref[...] = _silu(yc * lax.rsqrt(var + EPS) * lg_ref[...] + lb_ref[...]).astype(o_ref.dtype)


def _convmod(proj, row0, nb, seq, dw_w, dw_b, ln_g, ln_b, prev):
    t = min(256, seq)
    nc = seq // t
    rb = row0 // t
    vi = 0
    rows = lambda b, c: rb + b * nc + c
    full = lambda shape: pl.BlockSpec(shape, lambda b, c: (0, 0))
    return pl.pallas_call(
        functools.partial(_cm_body, t=t),
        grid=(nb, nc),
        in_specs=[
            pl.BlockSpec((t, D_MODEL), lambda b, c: (rows(b, c), vi)),
            pl.BlockSpec((t, D_MODEL), lambda b, c: (rows(b, c), vi + 1)),
            full((CM_K, D_MODEL)), full((1, D_MODEL)), full((1, D_MODEL)), full((1, D_MODEL)),
            pl.BlockSpec((1, CM_HIST, D_MODEL), lambda b, c: (b, 0, 0)),
        ],
        out_specs=[
            pl.BlockSpec((t, D_MODEL), lambda b, c: (b * nc + c, 0)),
            pl.BlockSpec((1, CM_HIST, D_MODEL), lambda b, c: (b, 0, 0)),
        ],
        out_shape=[
            jax.ShapeDtypeStruct((nb * seq, D_MODEL), BF16),
            jax.ShapeDtypeStruct((nb, CM_HIST, D_MODEL), F32),
        ],
        scratch_shapes=[pltpu.VMEM((CM_HIST + t, D_MODEL), F32), pltpu.VMEM((t, D_MODEL), F32)],
        compiler_params=_params(("parallel", "arbitrary")),
        name="convmod",
    )(proj, proj, dw_w, dw_b, ln_g, ln_b, prev)


def _attn_body(q_ref, k_ref, v_ref, o_ref, kb, vb, *, per_head):
    @pl.when(pl.program_id(1) == 0)
    def _():
        if per_head:
            for h in range(XA_HEADS):
                hs = slice(h * XA_DIM, (h + 1) * XA_DIM)
                kb[:, hs] = k_ref[0, :, h, :].astype(BF16)
                vb[:, hs] = v_ref[0, :, h, :].astype(BF16)
        else:
            kb[...] = k_ref[0].astype(BF16)
            vb[...] = v_ref[0].astype(BF16)

    nt = (((1,), (1,)), ((), ()))
    for h in range(XA_HEADS):
        hs = slice(h * XA_DIM, (h + 1) * XA_DIM)
        s = lax.dot_general(q_ref[:, hs], kb[:, hs], nt, preferred_element_type=F32) * (XA_DIM ** -0.5)
        e = jnp.exp(s - jnp.max(s, axis=-1, keepdims=True))
        p = (e / jnp.sum(e, axis=-1, keepdims=True)).astype(BF16)
        o_ref[:, hs] = jnp.dot(p, vb[:, hs], preferred_element_type=F32).astype(o_ref.dtype)


def _attend(q, row0, nb, seq, k, v):
    tq = min(512, seq)
    nq = seq // tq
    rb = row0 // tq
    per_head = k.ndim == 4
    if per_head:
        kv_spec = pl.BlockSpec((1, N_MEM, XA_HEADS, XA_DIM), lambda b, i: (b, 0, 0, 0))
    else:
        kv_spec = pl.BlockSpec((1, N_MEM, D_MODEL), lambda b, i: (b, 0, 0))
    return pl.pallas_call(
        functools.partial(_attn_body, per_head=per_head),
        grid=(nb, nq),
        in_specs=[pl.BlockSpec((tq, D_MODEL), lambda b, i: (rb + b * nq + i, 0)), kv_spec, kv_spec],
        out_specs=pl.BlockSpec((tq, D_MODEL), lambda b, i: (b * nq + i, 0)),
        out_shape=jax.ShapeDtypeStruct((nb * seq, D_MODEL), BF16),
        scratch_shapes=[pltpu.VMEM((N_MEM, D_MODEL), BF16), pltpu.VMEM((N_MEM, D_MODEL), BF16)],
        compiler_params=_params(("parallel", "arbitrary")),
        name="attend",
    )(q, k, v)


def _pad_rows_front(a, rows):
    return jnp.pad(a, ((0, 0), (rows - a.shape[1], 0), (0, 0)))


def kernel(x_prompt, x_sample, mem_prompt, cache_ssd_conv, cache_ssm_state, cache_cm_conv, cache_mem_k, cache_mem_v, ffn1_norm, ffn1_wi, ffn1_wo, mix_norm, w_in, ssd_conv_w, ssd_conv_b, ssd_dt_bias, ssd_a_log, ssd_d, ssd_norm, ssd_w_out, cm_dw_w, cm_dw_b, cm_ln_g, cm_ln_b, cm_w_out, w_mix_out, xa_norm, mem_norm, xa_wq, xa_wk, xa_wv, xa_wo, ffn2_norm, ffn2_wi, ffn2_wo, final_norm):
    assert ffn1_wi.shape[0] == 1, "single layer"
    bp, lp, _ = x_prompt.shape
    bs, ls, _ = x_sample.shape
    np_rows, ns_rows = bp * lp, bs * ls

    w = w_in[0]
    w_dt = jnp.pad(w[:, SSD_COLS:SSD_COLS + N_HEADS], ((0, 0), (0, LANES - N_HEADS)))
    w_rest = w[:, SSD_COLS + N_HEADS:].astype(BF16)
    cast = lambda a: a[0].astype(BF16)

    x1 = _ffn(x_prompt.reshape(np_rows, D_MODEL), x_sample.reshape(ns_rows, D_MODEL), ffn1_norm[0],
              cast(ffn1_wi), cast(ffn1_wo))
    proj = _norm_mm(x1, mix_norm[0], w, bm=1040, bn=1024, n=SSD_COLS, name="in_proj_ssd")
    proj_cm = _norm_mm(x1, mix_norm[0], w_rest, bm=1040, bn=1024, name="in_proj_cm")
    pdt = _norm_mm(x1, mix_norm[0], w_dt, bm=1040, bn=LANES, name="dt_proj")

    pad_h = lambda a: jnp.pad(a[0].reshape(1, N_HEADS), ((0, 0), (0, LANES - N_HEADS)))
    d_exp = jnp.repeat(ssd_d[0], HEADDIM).reshape(1, D_INNER)
    expand = (lax.broadcasted_iota(jnp.int32, (LANES, D_INNER), 1) // HEADDIM
              == lax.broadcasted_iota(jnp.int32, (LANES, D_INNER), 0)).astype(BF16)
    ssd_args = (ssd_conv_w[0], ssd_conv_b[0].reshape(1, XBC), pad_h(ssd_dt_bias), pad_h(ssd_a_log), d_exp,
                ssd_norm[0].reshape(1, D_INNER), jnp.concatenate([expand] * 3, axis=0))
    yn_p, hp = _ssd(proj, pdt, 0, bp, lp, *ssd_args,
                    jnp.zeros((bp, SSD_HIST, XBC), F32), jnp.zeros((bp, D_INNER, D_STATE), F32))
    yn_s, hs = _ssd(proj, pdt, np_rows, bs, ls, *ssd_args,
                    _pad_rows_front(cache_ssd_conv[0], SSD_HIST), cache_ssm_state[0].reshape(bs, D_INNER, D_STATE))
    xbc_cols = slice(D_INNER, D_INNER + XBC)
    p_ssd_conv = jnp.stack([proj[(b + 1) * lp - (SSD_K - 1):(b + 1) * lp, xbc_cols] for b in range(bp)])
    s_ssd_conv = jnp.concatenate(
        [cache_ssd_conv[0], proj[np_rows:, xbc_cols].reshape(bs, ls, XBC)], axis=1)[:, -(SSD_K - 1):]

    cm_args = (cm_dw_w[0], cm_dw_b[0].reshape(1, D_MODEL), cm_ln_g[0].reshape(1, D_MODEL),
               cm_ln_b[0].reshape(1, D_MODEL))
    un_p, cp = _convmod(proj_cm, 0, bp, lp, *cm_args, jnp.zeros((bp, CM_HIST, D_MODEL), F32))
    un_s, cs = _convmod(proj_cm, np_rows, bs, ls, *cm_args, _pad_rows_front(cache_cm_conv[0], CM_HIST))

    m = _merge(yn_p, yn_s, un_p, un_s, cast(ssd_w_out), cast(cm_w_out), proj_cm)
    x2 = _mm_res(m, None, w_mix_out[0], x1, bm=1040, bn=1024, name="mix_out")

    mem = mem_prompt.reshape(bp * N_MEM, D_MODEL)
    k_p = _norm_mm(mem, mem_norm[0], xa_wk[0], bm=bp * N_MEM, bn=1024, name="mem_k")
    v_p = _norm_mm(mem, mem_norm[0], xa_wv[0], bm=bp * N_MEM, bn=1024, name="mem_v")
    q = _norm_mm(x2, xa_norm[0], xa_wq[0], bm=1040, bn=1024, out_dtype=BF16, name="q_proj")
    o_p = _attend(q, 0, bp, lp, k_p.reshape(bp, N_MEM, D_MODEL), v_p.reshape(bp, N_MEM, D_MODEL))
    o_s = _attend(q, np_rows, bs, ls, cache_mem_k[0], cache_mem_v[0])
    x3 = _mm_res(o_p, o_s, xa_wo[0], x2, bm=640, bn=1024, name="attn_out")

    y_p, y_s = _ffn(x3, None, ffn2_norm[0], cast(ffn2_wi), cast(ffn2_wo), final_g=final_norm, split_rows=np_rows)

    kv_shape = (1, bp, N_MEM, XA_HEADS, XA_DIM)
    state_shape = lambda b: (1, b, N_HEADS, HEADDIM, D_STATE)
    return (
        y_p.reshape(bp, lp, D_MODEL),
        y_s.reshape(bs, ls, D_MODEL),
        p_ssd_conv[None],
        hp.reshape(state_shape(bp)),
        cp[:, CM_HIST - (CM_K - 1):][None],
        k_p.reshape(kv_shape),
        v_p.reshape(kv_shape),
        s_ssd_conv[None],
        hs.reshape(state_shape(bs)),
        cs[:, CM_HIST - (CM_K - 1):][None],
    )
```

```python
import functools

import jax
import jax.numpy as jnp
from jax import lax
from jax.experimental import pallas as pl
from jax.experimental.pallas import tpu as pltpu

F32 = jnp.float32
BF16 = jnp.bfloat16

D_MODEL = 2048
D_FF = 4 * D_MODEL
D_INNER = 2 * D_MODEL
HEADDIM = 64
N_HEADS = D_INNER // HEADDIM
N_GROUPS = 8
HEADS_PER_GROUP = N_HEADS // N_GROUPS
D_STATE = 128
GROUP_W = D_INNER // N_GROUPS
BC_W = N_GROUPS * D_STATE
SSD_K = 4
XBC = D_INNER + 2 * BC_W
CM_K = 31
CM_HIST = 32
SSD_HIST = 8
N_MEM = 256
XA_HEADS = 4
XA_DIM = D_MODEL // XA_HEADS
CHUNK = 64
EPS = 1e-6
LANES = 128
SUBLANES = 8
SSD_COLS = 2 * D_INNER + 2 * BC_W
VMEM_LIMIT = 56 * 1024 * 1024


def _params(sem):
    return pltpu.CompilerParams(dimension_semantics=sem, vmem_limit_bytes=VMEM_LIMIT)


def _rms(x, g):
    return x * lax.rsqrt(jnp.mean(x * x, axis=-1, keepdims=True) + EPS) * g


def _sigmoid(x):
    return 0.5 * jnp.tanh(0.5 * x) + 0.5


def _silu(x):
    return x * _sigmoid(x)


def _on_row_tiles(i, rows_main, full_fn, split_fn):
    last = pl.num_programs(0) - 1
    pl.when(i != last)(full_fn)
    pl.when(i == last)(lambda: split_fn(rows_main))


def _ffn_body(*refs, final, tail, split, r):
    it = iter(refs)
    x_ref = next(it)
    xt_ref = next(it) if tail else None
    g_ref, wa_ref, wb_ref, wo_ref = next(it), next(it), next(it), next(it)
    fg_ref = next(it) if final else None
    o_ref = next(it)
    ot_ref = next(it) if split else None
    h_ref, acc_ref = next(it), next(it)
    i, j = pl.program_id(0), pl.program_id(1)

    def norm_in(x):
        return _rms(x, g_ref[...]).astype(BF16)

    def finish(x, acc):
        y = x + 0.5 * acc
        return _rms(y, fg_ref[...]) if final else y

    @pl.when(j == 0)
    def _():
        acc_ref[...] = jnp.zeros_like(acc_ref)
        if tail:
            def full():
                h_ref[...] = norm_in(x_ref[...])

            def last(rm):
                h_ref[0:rm, :] = norm_in(x_ref[0:rm, :])
                h_ref[rm:, :] = norm_in(xt_ref[...])

            _on_row_tiles(i, r, full, last)
        else:
            h_ref[...] = norm_in(x_ref[...])

    h = h_ref[...]
    a = jnp.dot(h, wa_ref[...], preferred_element_type=F32)
    b = jnp.dot(h, wb_ref[...], preferred_element_type=F32)
    acc_ref[...] += jnp.dot((_silu(a) * b).astype(BF16), wo_ref[...], preferred_element_type=F32)

    @pl.when(j == pl.num_programs(1) - 1)
    def _():
        if tail or split:
            def full():
                o_ref[...] = finish(x_ref[...], acc_ref[...])

            def last(rm):
                o_ref[0:rm, :] = finish(x_ref[0:rm, :], acc_ref[0:rm, :])
                yt = finish(xt_ref[...] if tail else x_ref[rm:, :], acc_ref[rm:, :])
                if split:
                    ot_ref[...] = yt
                else:
                    o_ref[rm:, :] = yt

            _on_row_tiles(i, r, full, last)
        else:
            o_ref[...] = finish(x_ref[...], acc_ref[...])


def _ffn(x, x_tail, g, wi, wo, final_g=None, split_rows=None, *, bm=640, bf=512):
    d = x.shape[1]
    tail, split, final = x_tail is not None, split_rows is not None, final_g is not None
    m = x.shape[0] + (x_tail.shape[0] if tail else 0)
    n_main = x.shape[0] if tail else (split_rows if split else m)
    nm, nf = m // bm, D_FF // bf
    r = n_main - (nm - 1) * bm
    assert nm * bm == m and 0 < r <= bm and (r == bm or m - n_main == bm - r)
    row = lambda i, j: (i, 0)
    const = lambda i, j: (0, 0)
    in_specs = [pl.BlockSpec((bm, d), row)]
    args = [x]
    if tail:
        in_specs.append(pl.BlockSpec((bm - r, d), const))
        args.append(x_tail)
    in_specs += [
        pl.BlockSpec((1, d), const),
        pl.BlockSpec((d, bf), lambda i, j: (0, j)),
        pl.BlockSpec((d, bf), lambda i, j: (0, j + nf)),
        pl.BlockSpec((bf, d), lambda i, j: (j, 0)),
    ]
    args += [g.reshape(1, d), wi, wi, wo]
    if final:
        in_specs.append(pl.BlockSpec((1, d), const))
        args.append(final_g.reshape(1, d))
    if split:
        out_specs = [pl.BlockSpec((bm, d), row), pl.BlockSpec((bm - r, d), const)]
        out_shape = [jax.ShapeDtypeStruct((n_main, d), F32), jax.ShapeDtypeStruct((m - n_main, d), F32)]
    else:
        out_specs = pl.BlockSpec((bm, d), row)
        out_shape = jax.ShapeDtypeStruct((m, d), F32)
    return pl.pallas_call(
        functools.partial(_ffn_body, final=final, tail=tail, split=split, r=r),
        grid=(nm, nf),
        in_specs=in_specs,
        out_specs=out_specs,
        out_shape=out_shape,
        scratch_shapes=[pltpu.VMEM((bm, d), BF16), pltpu.VMEM((bm, d), F32)],
        compiler_params=_params(("arbitrary", "arbitrary")),
        name="ffn",
    )(*args)


def _norm_mm_body(x_ref, g_ref, w_ref, o_ref, h_ref, *, w_t):
    @pl.when(pl.program_id(1) == 0)
    def _():
        h_ref[...] = _rms(x_ref[...], g_ref[...]).astype(BF16)

    contract = (((1,), (1 if w_t else 0,)), ((), ()))
    o_ref[...] = lax.dot_general(h_ref[...], w_ref[...].astype(BF16), contract,
                                 preferred_element_type=F32).astype(o_ref.dtype)


def _norm_mm(x, g, w, *, bm, bn, w_t=False, n=None, out_dtype=F32, name="norm_mm"):
    m, d = x.shape
    if n is None:
        n = w.shape[0] if w_t else w.shape[1]
    w_spec = pl.BlockSpec((bn, d), lambda i, j: (j, 0)) if w_t else pl.BlockSpec((d, bn), lambda i, j: (0, j))
    return pl.pallas_call(
        functools.partial(_norm_mm_body, w_t=w_t),
        grid=(m // bm, n // bn),
        in_specs=[
            pl.BlockSpec((bm, d), lambda i, j: (i, 0)),
            pl.BlockSpec((1, d), lambda i, j: (0, 0)),
            w_spec,
        ],
        out_specs=pl.BlockSpec((bm, bn), lambda i, j: (i, j)),
        out_shape=jax.ShapeDtypeStruct((m, n), out_dtype),
        scratch_shapes=[pltpu.VMEM((bm, d), BF16)],
        compiler_params=_params(("parallel", "arbitrary")),
        name=name,
    )(x, g.reshape(1, d), w)


def _mm_res_body(a_ref, *rest, tail, r):
    at_ref = rest[0] if tail else None
    w_ref, r_ref, o_ref = rest[-3:]

    def tile(a, res):
        return res + jnp.dot(a, w_ref[...].astype(BF16), preferred_element_type=F32)

    if tail:
        def full():
            o_ref[...] = tile(a_ref[...], r_ref[...])

        def last(rm):
            o_ref[0:rm, :] = tile(a_ref[0:rm, :], r_ref[0:rm, :])
            o_ref[rm:, :] = tile(at_ref[...], r_ref[rm:, :])

        _on_row_tiles(pl.program_id(0), r, full, last)
    else:
        o_ref[...] = tile(a_ref[...], r_ref[...])


def _mm_res(a, a_tail, w, res, *, bm, bn, name):
    k = a.shape[1]
    m, n = res.shape
    tail = a_tail is not None
    nm = m // bm
    r = a.shape[0] - (nm - 1) * bm
    assert nm * bm == m and (not tail or a_tail.shape[0] == bm - r)
    in_specs = [pl.BlockSpec((bm, k), lambda i, j: (i, 0))]
    args = [a]
    if tail:
        in_specs.append(pl.BlockSpec((bm - r, k), lambda i, j: (0, 0)))
        args.append(a_tail)
    in_specs += [pl.BlockSpec((k, bn), lambda i, j: (0, j)), pl.BlockSpec((bm, bn), lambda i, j: (i, j))]
    return pl.pallas_call(
        functools.partial(_mm_res_body, tail=tail, r=r),
        grid=(nm, n // bn),
        in_specs=in_specs,
        out_specs=pl.BlockSpec((bm, bn), lambda i, j: (i, j)),
        out_shape=jax.ShapeDtypeStruct((m, n), F32),
        compiler_params=_params(("parallel", "arbitrary")),
        name=name,
    )(*args, w, res)


def _merge_body(y_ref, yt_ref, u_ref, ut_ref, ws_ref, wc_ref, gs_ref, gc_ref, o_ref, *, r):
    def tile(y, u, gs, gc):
        ssd = jnp.dot(y, ws_ref[...], preferred_element_type=F32)
        cm = jnp.dot(u, wc_ref[...], preferred_element_type=F32)
        return (_sigmoid(gs) * ssd + _sigmoid(gc) * cm).astype(o_ref.dtype)

    def full():
        o_ref[...] = tile(y_ref[...], u_ref[...], gs_ref[...], gc_ref[...])

    def last(rm):
        o_ref[0:rm, :] = tile(y_ref[0:rm, :], u_ref[0:rm, :], gs_ref[0:rm, :], gc_ref[0:rm, :])
        o_ref[rm:, :] = tile(yt_ref[...], ut_ref[...], gs_ref[rm:, :], gc_ref[rm:, :])

    _on_row_tiles(pl.program_id(0), r, full, last)


def _merge(yn, yn_tail, un, un_tail, w_ssd, w_cm, proj, *, bm=640, bn=512):
    m = proj.shape[0]
    nm = m // bm
    r = yn.shape[0] - (nm - 1) * bm
    assert nm * bm == m and yn_tail.shape[0] == bm - r
    gs_off = (2 * D_MODEL) // bn
    gc_off = (3 * D_MODEL) // bn
    return pl.pallas_call(
        functools.partial(_merge_body, r=r),
        grid=(nm, D_MODEL // bn),
        in_specs=[
            pl.BlockSpec((bm, D_INNER), lambda i, j: (i, 0)),
            pl.BlockSpec((bm - r, D_INNER), lambda i, j: (0, 0)),
            pl.BlockSpec((bm, D_MODEL), lambda i, j: (i, 0)),
            pl.BlockSpec((bm - r, D_MODEL), lambda i, j: (0, 0)),
            pl.BlockSpec((D_INNER, bn), lambda i, j: (0, j)),
            pl.BlockSpec((D_MODEL, bn), lambda i, j: (0, j)),
            pl.BlockSpec((bm, bn), lambda i, j: (i, j + gs_off)),
            pl.BlockSpec((bm, bn), lambda i, j: (i, j + gc_off)),
        ],
        out_specs=pl.BlockSpec((bm, bn), lambda i, j: (i, j)),
        out_shape=jax.ShapeDtypeStruct((m, D_MODEL), BF16),
        compiler_params=_params(("parallel", "arbitrary")),
        name="merge",
    )(yn, yn_tail, un, un_tail, w_ssd, w_cm, proj, proj)


def _split3(v):
    hi = v.astype(BF16)
    r = v - hi.astype(F32)
    mid = r.astype(BF16)
    lo = (r - mid.astype(F32)).astype(BF16)
    return hi, mid, lo


def _ssd_body(z_ref, xs_ref, b_ref, c_ref, dt_ref, wx_ref, wb_ref, wc_ref, bx_ref, bb_ref, bc_ref,
              dtb_ref, alog_ref, dexp_ref, ng_ref, e_ref, px_ref, pb_ref, pc_ref, h0_ref,
              y_ref, hout_ref, xhist, bhist, chist, h_ref, *, t):
    @pl.when(pl.program_id(1) == 0)
    def _():
        xhist[...] = px_ref[0]
        bhist[...] = pb_ref[0]
        chist[...] = pc_ref[0]
        for i in range(D_INNER // LANES):
            cs = slice(i * LANES, (i + 1) * LANES)
            h_ref[:, cs] = h0_ref[0, cs, :].T

    def conv_silu(hist, raw_ref, w_ref, bias_ref, cs):
        raw = raw_ref[:, cs]
        win = jnp.concatenate([hist[:, cs], raw], axis=0)
        acc = bias_ref[:, cs] + w_ref[SSD_K - 1:SSD_K, cs] * raw
        for k in range(SSD_K - 1):
            back = SSD_K - 1 - k
            acc = acc + w_ref[k:k + 1, cs] * pltpu.roll(win, t + back, 0)[0:t, :]
        hist[:, cs] = raw[t - SSD_HIST:t, :]
        return _silu(acc)

    every = slice(None)
    bm = conv_silu(bhist, b_ref, wb_ref, bb_ref, every)
    cm = conv_silu(chist, c_ref, wc_ref, bc_ref, every)

    dtp = jax.nn.softplus(dt_ref[...] + dtb_ref[...])
    d_a = dtp * (-jnp.exp(alog_ref[...]))
    row = lax.broadcasted_iota(jnp.int32, (t, LANES), 0)
    a_cs = d_a
    s = 1
    while s < t:
        a_cs = a_cs + jnp.where(row >= s, pltpu.roll(a_cs, s, 0), 0.0)
        s *= 2
    a_last = a_cs[t - 1:t, :]
    e_in = jnp.exp(a_cs)
    e_out = jnp.exp(a_last - a_cs)
    factors = jnp.concatenate(_split3(jnp.concatenate([dtp, e_in, e_out], axis=0)), axis=1)

    pad = jnp.zeros((LANES - t, LANES), F32)
    a_cs_t = jnp.concatenate([a_cs, pad], axis=0).T[:, 0:t]
    li = lax.broadcasted_iota(jnp.int32, (t, t), 0)
    si = lax.broadcasted_iota(jnp.int32, (t, t), 1)
    causal = li >= si

    nt = (((1,), (1,)), ((), ()))
    tn = (((0,), (0,)), ((), ()))
    for g in range(N_GROUPS):
        gs = slice(g * GROUP_W, (g + 1) * GROUP_W)
        ns = slice(g * D_STATE, (g + 1) * D_STATE)
        xs = conv_silu(xhist, xs_ref, wx_ref, bx_ref, gs)
        ex = jnp.dot(factors, e_ref[:, gs], preferred_element_type=F32)
        dt_e, ein_e, eout_e = ex[0:t], ex[t:2 * t], ex[2 * t:3 * t]
        xdt = xs * dt_e
        xdt_b = xdt.astype(BF16)
        xw_b = (xdt * eout_e).astype(BF16)

        c_g = cm[:, ns].astype(BF16)
        b_g = bm[:, ns].astype(BF16)
        cb = lax.dot_general(c_g, b_g, nt, preferred_element_type=F32)
        h_prev = h_ref[:, gs]
        y_off = jnp.dot(c_g, h_prev.astype(BF16), preferred_element_type=F32) * ein_e
        y_heads = []
        for r in range(HEADS_PER_GROUP):
            h = g * HEADS_PER_GROUP + r
            seg = a_cs[:, h:h + 1] - a_cs_t[h:h + 1, :]
            m_h = (cb * jnp.where(causal, jnp.exp(seg), 0.0)).astype(BF16)
            y_heads.append(jnp.dot(m_h, xdt_b[:, r * HEADDIM:(r + 1) * HEADDIM], preferred_element_type=F32))
        y = jnp.concatenate(y_heads, axis=1) + y_off
        states = lax.dot_general(b_g, xw_b, tn, preferred_element_type=F32)
        h_ref[:, gs] = h_prev * ein_e[t - 1:t, :] + states

        y = y + dexp_ref[:, gs] * xs
        v = y * _silu(z_ref[:, gs])
        y_ref[:, gs] = _rms(v, ng_ref[:, gs]).astype(y_ref.dtype)

    @pl.when(pl.program_id(1) == pl.num_programs(1) - 1)
    def _():
        for i in range(D_INNER // LANES):
            cs = slice(i * LANES, (i + 1) * LANES)
            hout_ref[0, cs, :] = h_ref[:, cs].T


def _ssd(proj, pdt, row0, nb, seq, conv_w, conv_b, dt_bias, a_log, d_exp, norm_g, expand, prev, h0):
    t = min(CHUNK, seq)
    nc = seq // t
    rb = row0 // t
    xi, bi, ci = D_INNER // D_INNER, (2 * D_INNER) // BC_W, (2 * D_INNER + BC_W) // BC_W
    rows = lambda b, c: rb + b * nc + c
    full = lambda shape: pl.BlockSpec(shape, lambda b, c: (0, 0))
    in_specs = [
        pl.BlockSpec((t, D_INNER), lambda b, c: (rows(b, c), 0)),
        pl.BlockSpec((t, D_INNER), lambda b, c: (rows(b, c), xi)),
        pl.BlockSpec((t, BC_W), lambda b, c: (rows(b, c), bi)),
        pl.BlockSpec((t, BC_W), lambda b, c: (rows(b, c), ci)),
        pl.BlockSpec((t, LANES), lambda b, c: (rows(b, c), 0)),
        pl.BlockSpec((SSD_K, D_INNER), lambda b, c: (0, 0)),
        pl.BlockSpec((SSD_K, BC_W), lambda b, c: (0, D_INNER // BC_W)),
        pl.BlockSpec((SSD_K, BC_W), lambda b, c: (0, D_INNER // BC_W + 1)),
        pl.BlockSpec((1, D_INNER), lambda b, c: (0, 0)),
        pl.BlockSpec((1, BC_W), lambda b, c: (0, D_INNER // BC_W)),
        pl.BlockSpec((1, BC_W), lambda b, c: (0, D_INNER // BC_W + 1)),
        full((1, LANES)), full((1, LANES)), full((1, D_INNER)), full((1, D_INNER)), full((3 * LANES, D_INNER)),
        pl.BlockSpec((1, SSD_HIST, D_INNER), lambda b, c: (b, 0, 0)),
        pl.BlockSpec((1, SSD_HIST, BC_W), lambda b, c: (b, 0, D_INNER // BC_W)),
        pl.BlockSpec((1, SSD_HIST, BC_W), lambda b, c: (b, 0, D_INNER // BC_W + 1)),
        pl.BlockSpec((1, D_INNER, D_STATE), lambda b, c: (b, 0, 0)),
    ]
    return pl.pallas_call(
        functools.partial(_ssd_body, t=t),
        grid=(nb, nc),
        in_specs=in_specs,
        out_specs=[
            pl.BlockSpec((t, D_INNER), lambda b, c: (b * nc + c, 0)),
            pl.BlockSpec((1, D_INNER, D_STATE), lambda b, c: (b, 0, 0)),
        ],
        out_shape=[
            jax.ShapeDtypeStruct((nb * seq, D_INNER), BF16),
            jax.ShapeDtypeStruct((nb, D_INNER, D_STATE), F32),
        ],
        scratch_shapes=[
            pltpu.VMEM((SSD_HIST, D_INNER), F32),
            pltpu.VMEM((SSD_HIST, BC_W), F32),
            pltpu.VMEM((SSD_HIST, BC_W), F32),
            pltpu.VMEM((D_STATE, D_INNER), F32),
        ],
        compiler_params=_params(("parallel", "arbitrary")),
        name="ssd",
    )(proj, proj, proj, proj, pdt, conv_w, conv_w, conv_w, conv_b, conv_b, conv_b,
      dt_bias, a_log, d_exp, norm_g, expand, prev, prev, prev, h0)


CM_ROWS = 64


def _cm_body(v_ref, g_ref, w_ref, b_ref, lg_ref, lb_ref, p_ref, o_ref, s_ref, buf, ybuf, *, t):
    @pl.when(pl.program_id(1) == 0)
    def _():
        buf[0:CM_HIST, :] = p_ref[0]

    buf[CM_HIST:CM_HIST + t, :] = v_ref[...] * _sigmoid(g_ref[...])
    base = CM_HIST - (CM_K - 1)
    rt = min(CM_ROWS, t)
    win_rows = rt + CM_HIST

    def cols(ci, carry):
        cs = pl.ds(pl.multiple_of(ci * LANES, LANES), LANES)
        for r0 in range(0, t, rt):
            win = buf[r0:r0 + win_rows, cs]
            acc = b_ref[:, cs] + w_ref[CM_K - 1:CM_K, cs] * buf[r0 + CM_HIST:r0 + CM_HIST + rt, cs]
            for s in range(SUBLANES):
                sh = win if s == 0 else pltpu.roll(win, win_rows - s, 0)
                for a in range(CM_HIST // SUBLANES):
                    k = SUBLANES * a + s - base
                    if 0 <= k < CM_K - 1:
                        acc = acc + w_ref[k:k + 1, cs] * sh[SUBLANES * a:SUBLANES * a + rt, :]
            ybuf[r0:r0 + rt, cs] = acc
        return carry

    lax.fori_loop(0, D_MODEL // LANES, cols, 0)

    hist = buf[t:t + CM_HIST, :]
    buf[0:CM_HIST, :] = hist

    @pl.when(pl.program_id(1) == pl.num_programs(1) - 1)
    def _():
        s_ref[0] = hist

    y = ybuf[...]
    mu = jnp.mean(y, axis=-1, keepdims=True)
    yc = y - mu
    var = jnp.mean(yc * yc, axis=-1, keepdims=True)
    o_ref[...] = _silu(yc * lax.rsqrt(var + EPS) * lg_ref[...] + lb_ref[...]).astype(o_ref.dtype)


def _convmod(proj, row0, nb, seq, dw_w, dw_b, ln_g, ln_b, prev):
    t = min(256, seq)
    nc = seq // t
    rb = row0 // t
    vi = 0
    rows = lambda b, c: rb + b * nc + c
    full = lambda shape: pl.BlockSpec(shape, lambda b, c: (0, 0))
    return pl.pallas_call(
        functools.partial(_cm_body, t=t),
        grid=(nb, nc),
        in_specs=[
            pl.BlockSpec((t, D_MODEL), lambda b, c: (rows(b, c), vi)),
            pl.BlockSpec((t, D_MODEL), lambda b, c: (rows(b, c), vi + 1)),
            full((CM_K, D_MODEL)), full((1, D_MODEL)), full((1, D_MODEL)), full((1, D_MODEL)),
            pl.BlockSpec((1, CM_HIST, D_MODEL), lambda b, c: (b, 0, 0)),
        ],
        out_specs=[
            pl.BlockSpec((t, D_MODEL), lambda b, c: (b * nc + c, 0)),
            pl.BlockSpec((1, CM_HIST, D_MODEL), lambda b, c: (b, 0, 0)),
        ],
        out_shape=[
            jax.ShapeDtypeStruct((nb * seq, D_MODEL), BF16),
            jax.ShapeDtypeStruct((nb, CM_HIST, D_MODEL), F32),
        ],
        scratch_shapes=[pltpu.VMEM((CM_HIST + t, D_MODEL), F32), pltpu.VMEM((t, D_MODEL), F32)],
        compiler_params=_params(("parallel", "arbitrary")),
        name="convmod",
    )(proj, proj, dw_w, dw_b, ln_g, ln_b, prev)


def _attn_body(q_ref, k_ref, v_ref, o_ref, kb, vb, *, per_head):
    @pl.when(pl.program_id(1) == 0)
    def _():
        if per_head:
            for h in range(XA_HEADS):
                hs = slice(h * XA_DIM, (h + 1) * XA_DIM)
                kb[:, hs] = k_ref[0, :, h, :].astype(BF16)
                vb[:, hs] = v_ref[0, :, h, :].astype(BF16)
        else:
            kb[...] = k_ref[0].astype(BF16)
            vb[...] = v_ref[0].astype(BF16)

    nt = (((1,), (1,)), ((), ()))
    for h in range(XA_HEADS):
        hs = slice(h * XA_DIM, (h + 1) * XA_DIM)
        s = lax.dot_general(q_ref[:, hs], kb[:, hs], nt, preferred_element_type=F32) * (XA_DIM ** -0.5)
        e = jnp.exp(s - jnp.max(s, axis=-1, keepdims=True))
        p = (e / jnp.sum(e, axis=-1, keepdims=True)).astype(BF16)
        o_ref[:, hs] = jnp.dot(p, vb[:, hs], preferred_element_type=F32).astype(o_ref.dtype)


def _attend(q, row0, nb, seq, k, v):
    tq = min(512, seq)
    nq = seq // tq
    rb = row0 // tq
    per_head = k.ndim == 4
    if per_head:
        kv_spec = pl.BlockSpec((1, N_MEM, XA_HEADS, XA_DIM), lambda b, i: (b, 0, 0, 0))
    else:
        kv_spec = pl.BlockSpec((1, N_MEM, D_MODEL), lambda b, i: (b, 0, 0))
    return pl.pallas_call(
        functools.partial(_attn_body, per_head=per_head),
        grid=(nb, nq),
        in_specs=[pl.BlockSpec((tq, D_MODEL), lambda b, i: (rb + b * nq + i, 0)), kv_spec, kv_spec],
        out_specs=pl.BlockSpec((tq, D_MODEL), lambda b, i: (b * nq + i, 0)),
        out_shape=jax.ShapeDtypeStruct((nb * seq, D_MODEL), BF16),
        scratch_shapes=[pltpu.VMEM((N_MEM, D_MODEL), BF16), pltpu.VMEM((N_MEM, D_MODEL), BF16)],
        compiler_params=_params(("parallel", "arbitrary")),
        name="attend",
    )(q, k, v)


def _pad_rows_front(a, rows):
    return jnp.pad(a, ((0, 0), (rows - a.shape[1], 0), (0, 0)))


def kernel(x_prompt, x_sample, mem_prompt, cache_ssd_conv, cache_ssm_state, cache_cm_conv, cache_mem_k, cache_mem_v, ffn1_norm, ffn1_wi, ffn1_wo, mix_norm, w_in, ssd_conv_w, ssd_conv_b, ssd_dt_bias, ssd_a_log, ssd_d, ssd_norm, ssd_w_out, cm_dw_w, cm_dw_b, cm_ln_g, cm_ln_b, cm_w_out, w_mix_out, xa_norm, mem_norm, xa_wq, xa_wk, xa_wv, xa_wo, ffn2_norm, ffn2_wi, ffn2_wo, final_norm):
    assert ffn1_wi.shape[0] == 1, "single layer"
    bp, lp, _ = x_prompt.shape
    bs, ls, _ = x_sample.shape
    np_rows, ns_rows = bp * lp, bs * ls

    w_t = jnp.transpose(w_in[0])
    w_t_b = w_t.astype(BF16)
    w_dt_t = jnp.pad(w_t[SSD_COLS:SSD_COLS + N_HEADS], ((0, LANES - N_HEADS), (0, 0)))
    w_cm_t = w_t_b[SSD_COLS + N_HEADS:]
    cast = lambda a: a[0].astype(BF16)

    x1 = _ffn(x_prompt.reshape(np_rows, D_MODEL), x_sample.reshape(ns_rows, D_MODEL), ffn1_norm[0],
              cast(ffn1_wi), cast(ffn1_wo))
    proj = _norm_mm(x1, mix_norm[0], w_t_b, bm=1040, bn=1024, w_t=True, n=SSD_COLS, name="in_proj_ssd")
    proj_cm = _norm_mm(x1, mix_norm[0], w_cm_t, bm=1040, bn=1024, w_t=True, name="in_proj_cm")
    pdt = _norm_mm(x1, mix_norm[0], w_dt_t, bm=1040, bn=LANES, w_t=True, name="dt_proj")

    pad_h = lambda a: jnp.pad(a[0].reshape(1, N_HEADS), ((0, 0), (0, LANES - N_HEADS)))
    d_exp = jnp.repeat(ssd_d[0], HEADDIM).reshape(1, D_INNER)
    expand = (lax.broadcasted_iota(jnp.int32, (LANES, D_INNER), 1) // HEADDIM
              == lax.broadcasted_iota(jnp.int32, (LANES, D_INNER), 0)).astype(BF16)
    ssd_args = (ssd_conv_w[0], ssd_conv_b[0].reshape(1, XBC), pad_h(ssd_dt_bias), pad_h(ssd_a_log), d_exp,
                ssd_norm[0].reshape(1, D_INNER), jnp.concatenate([expand] * 3, axis=0))
    yn_p, hp = _ssd(proj, pdt, 0, bp, lp, *ssd_args,
                    jnp.zeros((bp, SSD_HIST, XBC), F32), jnp.zeros((bp, D_INNER, D_STATE), F32))
    yn_s, hs = _ssd(proj, pdt, np_rows, bs, ls, *ssd_args,
                    _pad_rows_front(cache_ssd_conv[0], SSD_HIST), cache_ssm_state[0].reshape(bs, D_INNER, D_STATE))
    xbc_cols = slice(D_INNER, D_INNER + XBC)
    p_ssd_conv = jnp.stack([proj[(b + 1) * lp - (SSD_K - 1):(b + 1) * lp, xbc_cols] for b in range(bp)])
    s_ssd_conv = jnp.concatenate(
        [cache_ssd_conv[0], proj[np_rows:, xbc_cols].reshape(bs, ls, XBC)], axis=1)[:, -(SSD_K - 1):]

    cm_args = (cm_dw_w[0], cm_dw_b[0].reshape(1, D_MODEL), cm_ln_g[0].reshape(1, D_MODEL),
               cm_ln_b[0].reshape(1, D_MODEL))
    un_p, cp = _convmod(proj_cm, 0, bp, lp, *cm_args, jnp.zeros((bp, CM_HIST, D_MODEL), F32))
    un_s, cs = _convmod(proj_cm, np_rows, bs, ls, *cm_args, _pad_rows_front(cache_cm_conv[0], CM_HIST))

    m = _merge(yn_p, yn_s, un_p, un_s, cast(ssd_w_out), cast(cm_w_out), proj_cm)
    x2 = _mm_res(m, None, w_mix_out[0], x1, bm=1040, bn=1024, name="mix_out")

    mem = mem_prompt.reshape(bp * N_MEM, D_MODEL)
    k_p = _norm_mm(mem, mem_norm[0], xa_wk[0], bm=bp * N_MEM, bn=1024, name="mem_k")
    v_p = _norm_mm(mem, mem_norm[0], xa_wv[0], bm=bp * N_MEM, bn=1024, name="mem_v")
    q = _norm_mm(x2, xa_norm[0], xa_wq[0], bm=1040, bn=1024, out_dtype=BF16, name="q_proj")
    o_p = _attend(q, 0, bp, lp, k_p.reshape(bp, N_MEM, D_MODEL), v_p.reshape(bp, N_MEM, D_MODEL))
    o_s = _attend(q, np_rows, bs, ls, cache_mem_k[0], cache_mem_v[0])
    x3 = _mm_res(o_p, o_s, xa_wo[0], x2, bm=1040, bn=1024, name="attn_out")

    y_p, y_s = _ffn(x3, None, ffn2_norm[0], cast(ffn2_wi), cast(ffn2_wo), final_g=final_norm, split_rows=np_rows)

    kv_shape = (1, bp, N_MEM, XA_HEADS, XA_DIM)
    state_shape = lambda b: (1, b, N_HEADS, HEADDIM, D_STATE)
    return (
        y_p.reshape(bp, lp, D_MODEL),
        y_s.reshape(bs, ls, D_MODEL),
        p_ssd_conv[None],
        hp.reshape(state_shape(bp)),
        cp[:, CM_HIST - (CM_K - 1):][None],
        k_p.reshape(kv_shape),
        v_p.reshape(kv_shape),
        s_ssd_conv[None],
        hs.reshape(state_shape(bs)),
        cs[:, CM_HIST - (CM_K - 1):][None],
    )
```

```python
import functools

import jax
import jax.numpy as jnp
from jax import lax
from jax.experimental import pallas as pl
from jax.experimental.pallas import tpu as pltpu

F32 = jnp.float32
BF16 = jnp.bfloat16

D_MODEL = 2048
D_FF = 4 * D_MODEL
D_INNER = 2 * D_MODEL
HEADDIM = 64
N_HEADS = D_INNER // HEADDIM
N_GROUPS = 8
HEADS_PER_GROUP = N_HEADS // N_GROUPS
D_STATE = 128
GROUP_W = D_INNER // N_GROUPS
BC_W = N_GROUPS * D_STATE
SSD_K = 4
XBC = D_INNER + 2 * BC_W
CM_K = 31
CM_HIST = 32
SSD_HIST = 8
N_MEM = 256
XA_HEADS = 4
XA_DIM = D_MODEL // XA_HEADS
CHUNK = 64
EPS = 1e-6
LANES = 128
SUBLANES = 8
SSD_COLS = 2 * D_INNER + 2 * BC_W
VMEM_LIMIT = 56 * 1024 * 1024


def _params(sem):
    return pltpu.CompilerParams(dimension_semantics=sem, vmem_limit_bytes=VMEM_LIMIT)


def _rms(x, g):
    return x * lax.rsqrt(jnp.mean(x * x, axis=-1, keepdims=True) + EPS) * g


def _sigmoid(x):
    return 0.5 * jnp.tanh(0.5 * x) + 0.5


def _silu(x):
    return x * _sigmoid(x)


def _on_row_tiles(i, rows_main, full_fn, split_fn):
    last = pl.num_programs(0) - 1
    pl.when(i != last)(full_fn)
    pl.when(i == last)(lambda: split_fn(rows_main))


def _ffn_body(*refs, final, tail, split, r):
    it = iter(refs)
    x_ref = next(it)
    xt_ref = next(it) if tail else None
    g_ref, wa_ref, wb_ref, wo_ref = next(it), next(it), next(it), next(it)
    fg_ref = next(it) if final else None
    o_ref = next(it)
    ot_ref = next(it) if split else None
    h_ref = next(it)
    acc_ref = o_ref
    i, j = pl.program_id(0), pl.program_id(1)

    def norm_in(x):
        return _rms(x, g_ref[...]).astype(BF16)

    def finish(x, acc):
        y = x + 0.5 * acc
        return _rms(y, fg_ref[...]) if final else y

    @pl.when(j == 0)
    def _():
        acc_ref[...] = jnp.zeros_like(acc_ref)
        if tail:
            def full():
                h_ref[...] = norm_in(x_ref[...])

            def last(rm):
                h_ref[0:rm, :] = norm_in(x_ref[0:rm, :])
                h_ref[rm:, :] = norm_in(xt_ref[...])

            _on_row_tiles(i, r, full, last)
        else:
            h_ref[...] = norm_in(x_ref[...])

    h = h_ref[...]
    a = jnp.dot(h, wa_ref[...].astype(BF16), preferred_element_type=F32)
    b = jnp.dot(h, wb_ref[...].astype(BF16), preferred_element_type=F32)
    acc_ref[...] += jnp.dot((_silu(a) * b).astype(BF16), wo_ref[...].astype(BF16), preferred_element_type=F32)

    @pl.when(j == pl.num_programs(1) - 1)
    def _():
        if tail or split:
            def full():
                o_ref[...] = finish(x_ref[...], acc_ref[...])

            def last(rm):
                o_ref[0:rm, :] = finish(x_ref[0:rm, :], acc_ref[0:rm, :])
                yt = finish(xt_ref[...] if tail else x_ref[rm:, :], acc_ref[rm:, :])
                if split:
                    ot_ref[...] = yt
                else:
                    o_ref[rm:, :] = yt

            _on_row_tiles(i, r, full, last)
        else:
            o_ref[...] = finish(x_ref[...], acc_ref[...])


def _ffn(x, x_tail, g, wi, wo, final_g=None, split_rows=None, *, bm=1040, bf=256):
    d = x.shape[1]
    tail, split, final = x_tail is not None, split_rows is not None, final_g is not None
    m = x.shape[0] + (x_tail.shape[0] if tail else 0)
    n_main = x.shape[0] if tail else (split_rows if split else m)
    nm, nf = m // bm, D_FF // bf
    r = n_main - (nm - 1) * bm
    assert nm * bm == m and 0 < r <= bm and (r == bm or m - n_main == bm - r)
    row = lambda i, j: (i, 0)
    const = lambda i, j: (0, 0)
    in_specs = [pl.BlockSpec((bm, d), row, pipeline_mode=pl.Buffered(1))]
    args = [x]
    if tail:
        in_specs.append(pl.BlockSpec((bm - r, d), const))
        args.append(x_tail)
    in_specs += [
        pl.BlockSpec((1, d), const),
        pl.BlockSpec((d, bf), lambda i, j: (0, j)),
        pl.BlockSpec((d, bf), lambda i, j: (0, j + nf)),
        pl.BlockSpec((bf, d), lambda i, j: (j, 0)),
    ]
    args += [g.reshape(1, d), wi, wi, wo]
    if final:
        in_specs.append(pl.BlockSpec((1, d), const))
        args.append(final_g.reshape(1, d))
    if split:
        out_specs = [pl.BlockSpec((bm, d), row), pl.BlockSpec((bm - r, d), const)]
        out_shape = [jax.ShapeDtypeStruct((n_main, d), F32), jax.ShapeDtypeStruct((m - n_main, d), F32)]
    else:
        out_specs = pl.BlockSpec((bm, d), row)
        out_shape = jax.ShapeDtypeStruct((m, d), F32)
    return pl.pallas_call(
        functools.partial(_ffn_body, final=final, tail=tail, split=split, r=r),
        grid=(nm, nf),
        in_specs=in_specs,
        out_specs=out_specs,
        out_shape=out_shape,
        scratch_shapes=[pltpu.VMEM((bm, d), BF16)],
        compiler_params=_params(("arbitrary", "arbitrary")),
        name="ffn",
    )(*args)


def _norm_mm_body(x_ref, g_ref, w_ref, o_ref, h_ref, *, w_t):
    @pl.when(pl.program_id(1) == 0)
    def _():
        h_ref[...] = _rms(x_ref[...], g_ref[...]).astype(BF16)

    contract = (((1,), (1 if w_t else 0,)), ((), ()))
    o_ref[...] = lax.dot_general(h_ref[...], w_ref[...].astype(BF16), contract,
                                 preferred_element_type=F32).astype(o_ref.dtype)


def _norm_mm(x, g, w, *, bm, bn, w_t=False, n=None, out_dtype=F32, name="norm_mm"):
    m, d = x.shape
    if n is None:
        n = w.shape[0] if w_t else w.shape[1]
    w_spec = pl.BlockSpec((bn, d), lambda i, j: (j, 0)) if w_t else pl.BlockSpec((d, bn), lambda i, j: (0, j))
    return pl.pallas_call(
        functools.partial(_norm_mm_body, w_t=w_t),
        grid=(m // bm, n // bn),
        in_specs=[
            pl.BlockSpec((bm, d), lambda i, j: (i, 0)),
            pl.BlockSpec((1, d), lambda i, j: (0, 0)),
            w_spec,
        ],
        out_specs=pl.BlockSpec((bm, bn), lambda i, j: (i, j)),
        out_shape=jax.ShapeDtypeStruct((m, n), out_dtype),
        scratch_shapes=[pltpu.VMEM((bm, d), BF16)],
        compiler_params=_params(("parallel", "arbitrary")),
        name=name,
    )(x, g.reshape(1, d), w)


def _mm_res_body(a_ref, *rest, tail, r):
    at_ref = rest[0] if tail else None
    w_ref, r_ref, o_ref = rest[-3:]

    def tile(a, res):
        return res + jnp.dot(a, w_ref[...].astype(BF16), preferred_element_type=F32)

    if tail:
        def full():
            o_ref[...] = tile(a_ref[...], r_ref[...])

        def last(rm):
            o_ref[0:rm, :] = tile(a_ref[0:rm, :], r_ref[0:rm, :])
            o_ref[rm:, :] = tile(at_ref[...], r_ref[rm:, :])

        _on_row_tiles(pl.program_id(0), r, full, last)
    else:
        o_ref[...] = tile(a_ref[...], r_ref[...])


def _mm_res(a, a_tail, w, res, *, bm, bn, name):
    k = a.shape[1]
    m, n = res.shape
    tail = a_tail is not None
    nm = m // bm
    r = a.shape[0] - (nm - 1) * bm
    assert nm * bm == m and (not tail or a_tail.shape[0] == bm - r)
    in_specs = [pl.BlockSpec((bm, k), lambda i, j: (i, 0))]
    args = [a]
    if tail:
        in_specs.append(pl.BlockSpec((bm - r, k), lambda i, j: (0, 0)))
        args.append(a_tail)
    in_specs += [pl.BlockSpec((k, bn), lambda i, j: (0, j)), pl.BlockSpec((bm, bn), lambda i, j: (i, j))]
    return pl.pallas_call(
        functools.partial(_mm_res_body, tail=tail, r=r),
        grid=(nm, n // bn),
        in_specs=in_specs,
        out_specs=pl.BlockSpec((bm, bn), lambda i, j: (i, j)),
        out_shape=jax.ShapeDtypeStruct((m, n), F32),
        compiler_params=_params(("parallel", "arbitrary")),
        name=name,
    )(*args, w, res)


def _merge_body(y_ref, yt_ref, u_ref, ut_ref, ws_ref, wc_ref, gs_ref, gc_ref, o_ref, *, r):
    def tile(y, u, gs, gc):
        ssd = jnp.dot(y, ws_ref[...], preferred_element_type=F32)
        cm = jnp.dot(u, wc_ref[...], preferred_element_type=F32)
        return (_sigmoid(gs) * ssd + _sigmoid(gc) * cm).astype(o_ref.dtype)

    def full():
        o_ref[...] = tile(y_ref[...], u_ref[...], gs_ref[...], gc_ref[...])

    def last(rm):
        o_ref[0:rm, :] = tile(y_ref[0:rm, :], u_ref[0:rm, :], gs_ref[0:rm, :], gc_ref[0:rm, :])
        o_ref[rm:, :] = tile(yt_ref[...], ut_ref[...], gs_ref[rm:, :], gc_ref[rm:, :])

    _on_row_tiles(pl.program_id(0), r, full, last)


def _merge(yn, yn_tail, un, un_tail, w_ssd, w_cm, proj, *, bm=640, bn=512):
    m = proj.shape[0]
    nm = m // bm
    r = yn.shape[0] - (nm - 1) * bm
    assert nm * bm == m and yn_tail.shape[0] == bm - r
    gs_off = (2 * D_MODEL) // bn
    gc_off = (3 * D_MODEL) // bn
    return pl.pallas_call(
        functools.partial(_merge_body, r=r),
        grid=(nm, D_MODEL // bn),
        in_specs=[
            pl.BlockSpec((bm, D_INNER), lambda i, j: (i, 0)),
            pl.BlockSpec((bm - r, D_INNER), lambda i, j: (0, 0)),
            pl.BlockSpec((bm, D_MODEL), lambda i, j: (i, 0)),
            pl.BlockSpec((bm - r, D_MODEL), lambda i, j: (0, 0)),
            pl.BlockSpec((D_INNER, bn), lambda i, j: (0, j)),
            pl.BlockSpec((D_MODEL, bn), lambda i, j: (0, j)),
            pl.BlockSpec((bm, bn), lambda i, j: (i, j + gs_off)),
            pl.BlockSpec((bm, bn), lambda i, j: (i, j + gc_off)),
        ],
        out_specs=pl.BlockSpec((bm, bn), lambda i, j: (i, j)),
        out_shape=jax.ShapeDtypeStruct((m, D_MODEL), BF16),
        compiler_params=_params(("parallel", "arbitrary")),
        name="merge",
    )(yn, yn_tail, un, un_tail, w_ssd, w_cm, proj, proj)


def _split3(v):
    hi = v.astype(BF16)
    r = v - hi.astype(F32)
    mid = r.astype(BF16)
    lo = (r - mid.astype(F32)).astype(BF16)
    return hi, mid, lo


def _ssd_body(z_ref, xs_ref, b_ref, c_ref, dt_ref, wx_ref, wb_ref, wc_ref, bx_ref, bb_ref, bc_ref,
              dtb_ref, alog_ref, dexp_ref, ng_ref, e_ref, px_ref, pb_ref, pc_ref, h0_ref,
              y_ref, hout_ref, xhist, bhist, chist, h_ref, *, t):
    @pl.when(pl.program_id(1) == 0)
    def _():
        xhist[...] = px_ref[0]
        bhist[...] = pb_ref[0]
        chist[...] = pc_ref[0]
        for i in range(D_INNER // LANES):
            cs = slice(i * LANES, (i + 1) * LANES)
            h_ref[:, cs] = h0_ref[0, cs, :].T

    def conv_silu(hist, raw_ref, w_ref, bias_ref, cs):
        raw = raw_ref[:, cs]
        win = jnp.concatenate([hist[:, cs], raw], axis=0)
        acc = bias_ref[:, cs] + w_ref[SSD_K - 1:SSD_K, cs] * raw
        for k in range(SSD_K - 1):
            back = SSD_K - 1 - k
            acc = acc + w_ref[k:k + 1, cs] * pltpu.roll(win, t + back, 0)[0:t, :]
        hist[:, cs] = raw[t - SSD_HIST:t, :]
        return _silu(acc)

    every = slice(None)
    bm = conv_silu(bhist, b_ref, wb_ref, bb_ref, every)
    cm = conv_silu(chist, c_ref, wc_ref, bc_ref, every)

    dtp = jax.nn.softplus(dt_ref[...] + dtb_ref[...])
    d_a = dtp * (-jnp.exp(alog_ref[...]))
    row = lax.broadcasted_iota(jnp.int32, (t, LANES), 0)
    a_cs = d_a
    s = 1
    while s < t:
        a_cs = a_cs + jnp.where(row >= s, pltpu.roll(a_cs, s, 0), 0.0)
        s *= 2
    a_last = a_cs[t - 1:t, :]
    e_in = jnp.exp(a_cs)
    e_out = jnp.exp(a_last - a_cs)
    factors = jnp.concatenate(_split3(jnp.concatenate([dtp, e_in, e_out], axis=0)), axis=1)

    pad = jnp.zeros((LANES - t, LANES), F32)
    a_cs_t = jnp.concatenate([a_cs, pad], axis=0).T[:, 0:t]
    li = lax.broadcasted_iota(jnp.int32, (t, t), 0)
    si = lax.broadcasted_iota(jnp.int32, (t, t), 1)
    causal = li >= si

    nt = (((1,), (1,)), ((), ()))
    tn = (((0,), (0,)), ((), ()))
    for g in range(N_GROUPS):
        gs = slice(g * GROUP_W, (g + 1) * GROUP_W)
        ns = slice(g * D_STATE, (g + 1) * D_STATE)
        xs = conv_silu(xhist, xs_ref, wx_ref, bx_ref, gs)
        ex = jnp.dot(factors, e_ref[:, gs], preferred_element_type=F32)
        dt_e, ein_e, eout_e = ex[0:t], ex[t:2 * t], ex[2 * t:3 * t]
        xdt = xs * dt_e
        xdt_b = xdt.astype(BF16)
        xw_b = (xdt * eout_e).astype(BF16)

        c_g = cm[:, ns].astype(BF16)
        b_g = bm[:, ns].astype(BF16)
        cb = lax.dot_general(c_g, b_g, nt, preferred_element_type=F32)
        h_prev = h_ref[:, gs]
        y_off = jnp.dot(c_g, h_prev.astype(BF16), preferred_element_type=F32) * ein_e
        y_heads = []
        for r in range(HEADS_PER_GROUP):
            h = g * HEADS_PER_GROUP + r
            seg = a_cs[:, h:h + 1] - a_cs_t[h:h + 1, :]
            m_h = (cb * jnp.where(causal, jnp.exp(seg), 0.0)).astype(BF16)
            y_heads.append(jnp.dot(m_h, xdt_b[:, r * HEADDIM:(r + 1) * HEADDIM], preferred_element_type=F32))
        y = jnp.concatenate(y_heads, axis=1) + y_off
        states = lax.dot_general(b_g, xw_b, tn, preferred_element_type=F32)
        h_ref[:, gs] = h_prev * ein_e[t - 1:t, :] + states

        y = y + dexp_ref[:, gs] * xs
        v = y * _silu(z_ref[:, gs])
        y_ref[:, gs] = _rms(v, ng_ref[:, gs]).astype(y_ref.dtype)

    @pl.when(pl.program_id(1) == pl.num_programs(1) - 1)
    def _():
        for i in range(D_INNER // LANES):
            cs = slice(i * LANES, (i + 1) * LANES)
            hout_ref[0, cs, :] = h_ref[:, cs].T


def _ssd(proj, pdt, row0, nb, seq, conv_w, conv_b, dt_bias, a_log, d_exp, norm_g, expand, prev, h0):
    t = min(CHUNK, seq)
    nc = seq // t
    rb = row0 // t
    xi, bi, ci = D_INNER // D_INNER, (2 * D_INNER) // BC_W, (2 * D_INNER + BC_W) // BC_W
    rows = lambda b, c: rb + b * nc + c
    full = lambda shape: pl.BlockSpec(shape, lambda b, c: (0, 0))
    in_specs = [
        pl.BlockSpec((t, D_INNER), lambda b, c: (rows(b, c), 0)),
        pl.BlockSpec((t, D_INNER), lambda b, c: (rows(b, c), xi)),
        pl.BlockSpec((t, BC_W), lambda b, c: (rows(b, c), bi)),
        pl.BlockSpec((t, BC_W), lambda b, c: (rows(b, c), ci)),
        pl.BlockSpec((t, LANES), lambda b, c: (rows(b, c), 0)),
        pl.BlockSpec((SSD_K, D_INNER), lambda b, c: (0, 0)),
        pl.BlockSpec((SSD_K, BC_W), lambda b, c: (0, D_INNER // BC_W)),
        pl.BlockSpec((SSD_K, BC_W), lambda b, c: (0, D_INNER // BC_W + 1)),
        pl.BlockSpec((1, D_INNER), lambda b, c: (0, 0)),
        pl.BlockSpec((1, BC_W), lambda b, c: (0, D_INNER // BC_W)),
        pl.BlockSpec((1, BC_W), lambda b, c: (0, D_INNER // BC_W + 1)),
        full((1, LANES)), full((1, LANES)), full((1, D_INNER)), full((1, D_INNER)), full((3 * LANES, D_INNER)),
        pl.BlockSpec((1, SSD_HIST, D_INNER), lambda b, c: (b, 0, 0)),
        pl.BlockSpec((1, SSD_HIST, BC_W), lambda b, c: (b, 0, D_INNER // BC_W)),
        pl.BlockSpec((1, SSD_HIST, BC_W), lambda b, c: (b, 0, D_INNER // BC_W + 1)),
        pl.BlockSpec((1, D_INNER, D_STATE), lambda b, c: (b, 0, 0)),
    ]
    return pl.pallas_call(
        functools.partial(_ssd_body, t=t),
        grid=(nb, nc),
        in_specs=in_specs,
        out_specs=[
            pl.BlockSpec((t, D_INNER), lambda b, c: (b * nc + c, 0)),
            pl.BlockSpec((1, D_INNER, D_STATE), lambda b, c: (b, 0, 0)),
        ],
        out_shape=[
            jax.ShapeDtypeStruct((nb * seq, D_INNER), BF16),
            jax.ShapeDtypeStruct((nb, D_INNER, D_STATE), F32),
        ],
        scratch_shapes=[
            pltpu.VMEM((SSD_HIST, D_INNER), F32),
            pltpu.VMEM((SSD_HIST, BC_W), F32),
            pltpu.VMEM((SSD_HIST, BC_W), F32),
            pltpu.VMEM((D_STATE, D_INNER), F32),
        ],
        compiler_params=_params(("parallel", "arbitrary")),
        name="ssd",
    )(proj, proj, proj, proj, pdt, conv_w, conv_w, conv_w, conv_b, conv_b, conv_b,
      dt_bias, a_log, d_exp, norm_g, expand, prev, prev, prev, h0)


CM_ROWS = 64


def _cm_body(v_ref, g_ref, w_ref, b_ref, lg_ref, lb_ref, p_ref, o_ref, s_ref, buf, ybuf, *, t):
    @pl.when(pl.program_id(1) == 0)
    def _():
        buf[0:CM_HIST, :] = p_ref[0]

    buf[CM_HIST:CM_HIST + t, :] = v_ref[...] * _sigmoid(g_ref[...])
    base = CM_HIST - (CM_K - 1)
    rt = min(CM_ROWS, t)
    win_rows = rt + CM_HIST

    def cols(ci, carry):
        cs = pl.ds(pl.multiple_of(ci * LANES, LANES), LANES)
        for r0 in range(0, t, rt):
            win = buf[r0:r0 + win_rows, cs]
            acc = b_ref[:, cs] + w_ref[CM_K - 1:CM_K, cs] * buf[r0 + CM_HIST:r0 + CM_HIST + rt, cs]
            for s in range(SUBLANES):
                sh = win if s == 0 else pltpu.roll(win, win_rows - s, 0)
                for a in range(CM_HIST // SUBLANES):
                    k = SUBLANES * a + s - base
                    if 0 <= k < CM_K - 1:
                        acc = acc + w_ref[k:k + 1, cs] * sh[SUBLANES * a:SUBLANES * a + rt, :]
            ybuf[r0:r0 + rt, cs] = acc
        return carry

    lax.fori_loop(0, D_MODEL // LANES, cols, 0)

    hist = buf[t:t + CM_HIST, :]
    buf[0:CM_HIST, :] = hist

    @pl.when(pl.program_id(1) == pl.num_programs(1) - 1)
    def _():
        s_ref[0] = hist

    y = ybuf[...]
    mu = jnp.mean(y, axis=-1, keepdims=True)
    yc = y - mu
    var = jnp.mean(yc * yc, axis=-1, keepdims=True)
    o_ref[...] = _silu(yc * lax.rsqrt(var + EPS) * lg_ref[...] + lb_ref[...]).astype(o_ref.dtype)


def _convmod(proj, row0, nb, seq, dw_w, dw_b, ln_g, ln_b, prev):
    t = min(256, seq)
    nc = seq // t
    rb = row0 // t
    vi = 0
    rows = lambda b, c: rb + b * nc + c
    full = lambda shape: pl.BlockSpec(shape, lambda b, c: (0, 0))
    return pl.pallas_call(
        functools.partial(_cm_body, t=t),
        grid=(nb, nc),
        in_specs=[
            pl.BlockSpec((t, D_MODEL), lambda b, c: (rows(b, c), vi)),
            pl.BlockSpec((t, D_MODEL), lambda b, c: (rows(b, c), vi + 1)),
            full((CM_K, D_MODEL)), full((1, D_MODEL)), full((1, D_MODEL)), full((1, D_MODEL)),
            pl.BlockSpec((1, CM_HIST, D_MODEL), lambda b, c: (b, 0, 0)),
        ],
        out_specs=[
            pl.BlockSpec((t, D_MODEL), lambda b, c: (b * nc + c, 0)),
            pl.BlockSpec((1, CM_HIST, D_MODEL), lambda b, c: (b, 0, 0)),
        ],
        out_shape=[
            jax.ShapeDtypeStruct((nb * seq, D_MODEL), BF16),
            jax.ShapeDtypeStruct((nb, CM_HIST, D_MODEL), F32),
        ],
        scratch_shapes=[pltpu.VMEM((CM_HIST + t, D_MODEL), F32), pltpu.VMEM((t, D_MODEL), F32)],
        compiler_params=_params(("parallel", "arbitrary")),
        name="convmod",
    )(proj, proj, dw_w, dw_b, ln_g, ln_b, prev)


def _attn_body(q_ref, k_ref, v_ref, o_ref, kb, vb, *, per_head):
    @pl.when(pl.program_id(1) == 0)
    def _():
        if per_head:
            for h in range(XA_HEADS):
                hs = slice(h * XA_DIM, (h + 1) * XA_DIM)
                kb[:, hs] = k_ref[0, :, h, :].astype(BF16)
                vb[:, hs] = v_ref[0, :, h, :].astype(BF16)
        else:
            kb[...] = k_ref[0].astype(BF16)
            vb[...] = v_ref[0].astype(BF16)

    nt = (((1,), (1,)), ((), ()))
    for h in range(XA_HEADS):
        hs = slice(h * XA_DIM, (h + 1) * XA_DIM)
        s = lax.dot_general(q_ref[:, hs], kb[:, hs], nt, preferred_element_type=F32) * (XA_DIM ** -0.5)
        e = jnp.exp(s - jnp.max(s, axis=-1, keepdims=True))
        p = (e / jnp.sum(e, axis=-1, keepdims=True)).astype(BF16)
        o_ref[:, hs] = jnp.dot(p, vb[:, hs], preferred_element_type=F32).astype(o_ref.dtype)


def _attend(q, row0, nb, seq, k, v):
    tq = min(512, seq)
    nq = seq // tq
    rb = row0 // tq
    per_head = k.ndim == 4
    if per_head:
        kv_spec = pl.BlockSpec((1, N_MEM, XA_HEADS, XA_DIM), lambda b, i: (b, 0, 0, 0))
    else:
        kv_spec = pl.BlockSpec((1, N_MEM, D_MODEL), lambda b, i: (b, 0, 0))
    return pl.pallas_call(
        functools.partial(_attn_body, per_head=per_head),
        grid=(nb, nq),
        in_specs=[pl.BlockSpec((tq, D_MODEL), lambda b, i: (rb + b * nq + i, 0)), kv_spec, kv_spec],
        out_specs=pl.BlockSpec((tq, D_MODEL), lambda b, i: (b * nq + i, 0)),
        out_shape=jax.ShapeDtypeStruct((nb * seq, D_MODEL), BF16),
        scratch_shapes=[pltpu.VMEM((N_MEM, D_MODEL), BF16), pltpu.VMEM((N_MEM, D_MODEL), BF16)],
        compiler_params=_params(("parallel", "arbitrary")),
        name="attend",
    )(q, k, v)


def _pad_rows_front(a, rows):
    return jnp.pad(a, ((0, 0), (rows - a.shape[1], 0), (0, 0)))


def kernel(x_prompt, x_sample, mem_prompt, cache_ssd_conv, cache_ssm_state, cache_cm_conv, cache_mem_k, cache_mem_v, ffn1_norm, ffn1_wi, ffn1_wo, mix_norm, w_in, ssd_conv_w, ssd_conv_b, ssd_dt_bias, ssd_a_log, ssd_d, ssd_norm, ssd_w_out, cm_dw_w, cm_dw_b, cm_ln_g, cm_ln_b, cm_w_out, w_mix_out, xa_norm, mem_norm, xa_wq, xa_wk, xa_wv, xa_wo, ffn2_norm, ffn2_wi, ffn2_wo, final_norm):
    assert ffn1_wi.shape[0] == 1, "single layer"
    bp, lp, _ = x_prompt.shape
    bs, ls, _ = x_sample.shape
    np_rows, ns_rows = bp * lp, bs * ls

    w_t = jnp.transpose(w_in[0])
    w_t_b = w_t.astype(BF16)
    w_dt_t = jnp.pad(w_t[SSD_COLS:SSD_COLS + N_HEADS], ((0, LANES - N_HEADS), (0, 0)))
    w_cm_t = w_t_b[SSD_COLS + N_HEADS:]
    cast = lambda a: a[0].astype(BF16)

    x1 = _ffn(x_prompt.reshape(np_rows, D_MODEL), x_sample.reshape(ns_rows, D_MODEL), ffn1_norm[0],
              ffn1_wi[0], ffn1_wo[0])
    proj = _norm_mm(x1, mix_norm[0], w_t_b, bm=1040, bn=1024, w_t=True, n=SSD_COLS, name="in_proj_ssd")
    proj_cm = _norm_mm(x1, mix_norm[0], w_cm_t, bm=1040, bn=1024, w_t=True, name="in_proj_cm")
    pdt = _norm_mm(x1, mix_norm[0], w_dt_t, bm=1040, bn=LANES, w_t=True, name="dt_proj")

    pad_h = lambda a: jnp.pad(a[0].reshape(1, N_HEADS), ((0, 0), (0, LANES - N_HEADS)))
    d_exp = jnp.repeat(ssd_d[0], HEADDIM).reshape(1, D_INNER)
    expand = (lax.broadcasted_iota(jnp.int32, (LANES, D_INNER), 1) // HEADDIM
              == lax.broadcasted_iota(jnp.int32, (LANES, D_INNER), 0)).astype(BF16)
    ssd_args = (ssd_conv_w[0], ssd_conv_b[0].reshape(1, XBC), pad_h(ssd_dt_bias), pad_h(ssd_a_log), d_exp,
                ssd_norm[0].reshape(1, D_INNER), jnp.concatenate([expand] * 3, axis=0))
    yn_p, hp = _ssd(proj, pdt, 0, bp, lp, *ssd_args,
                    jnp.zeros((bp, SSD_HIST, XBC), F32), jnp.zeros((bp, D_INNER, D_STATE), F32))
    yn_s, hs = _ssd(proj, pdt, np_rows, bs, ls, *ssd_args,
                    _pad_rows_front(cache_ssd_conv[0], SSD_HIST), cache_ssm_state[0].reshape(bs, D_INNER, D_STATE))
    xbc_cols = slice(D_INNER, D_INNER + XBC)
    p_ssd_conv = jnp.stack([proj[(b + 1) * lp - (SSD_K - 1):(b + 1) * lp, xbc_cols] for b in range(bp)])
    s_ssd_conv = jnp.concatenate(
        [cache_ssd_conv[0], proj[np_rows:, xbc_cols].reshape(bs, ls, XBC)], axis=1)[:, -(SSD_K - 1):]

    cm_args = (cm_dw_w[0], cm_dw_b[0].reshape(1, D_MODEL), cm_ln_g[0].reshape(1, D_MODEL),
               cm_ln_b[0].reshape(1, D_MODEL))
    un_p, cp = _convmod(proj_cm, 0, bp, lp, *cm_args, jnp.zeros((bp, CM_HIST, D_MODEL), F32))
    un_s, cs = _convmod(proj_cm, np_rows, bs, ls, *cm_args, _pad_rows_front(cache_cm_conv[0], CM_HIST))

    m = _merge(yn_p, yn_s, un_p, un_s, cast(ssd_w_out), cast(cm_w_out), proj_cm)
    x2 = _mm_res(m, None, w_mix_out[0], x1, bm=1040, bn=1024, name="mix_out")

    mem = mem_prompt.reshape(bp * N_MEM, D_MODEL)
    k_p = _norm_mm(mem, mem_norm[0], xa_wk[0], bm=bp * N_MEM, bn=1024, name="mem_k")
    v_p = _norm_mm(mem, mem_norm[0], xa_wv[0], bm=bp * N_MEM, bn=1024, name="mem_v")
    q = _norm_mm(x2, xa_norm[0], xa_wq[0], bm=1040, bn=1024, out_dtype=BF16, name="q_proj")
    o_p = _attend(q, 0, bp, lp, k_p.reshape(bp, N_MEM, D_MODEL), v_p.reshape(bp, N_MEM, D_MODEL))
    o_s = _attend(q, np_rows, bs, ls, cache_mem_k[0], cache_mem_v[0])
    x3 = _mm_res(o_p, o_s, xa_wo[0], x2, bm=1040, bn=1024, name="attn_out")

    y_p, y_s = _ffn(x3, None, ffn2_norm[0], ffn2_wi[0], ffn2_wo[0], final_g=final_norm, split_rows=np_rows)

    kv_shape = (1, bp, N_MEM, XA_HEADS, XA_DIM)
    state_shape = lambda b: (1, b, N_HEADS, HEADDIM, D_STATE)
    return (
        y_p.reshape(bp, lp, D_MODEL),
        y_s.reshape(bs, ls, D_MODEL),
        p_ssd_conv[None],
        hp.reshape(state_shape(bp)),
        cp[:, CM_HIST - (CM_K - 1):][None],
        k_p.reshape(kv_shape),
        v_p.reshape(kv_shape),
        s_ssd_conv[None],
        hs.reshape(state_shape(bs)),
        cs[:, CM_HIST - (CM_K - 1):][None],
    )
```

```python
import functools

import jax
import jax.numpy as jnp
from jax import lax
from jax.experimental import pallas as pl
from jax.experimental.pallas import tpu as pltpu

F32 = jnp.float32
BF16 = jnp.bfloat16

D_MODEL = 2048
D_FF = 4 * D_MODEL
D_INNER = 2 * D_MODEL
HEADDIM = 64
N_HEADS = D_INNER // HEADDIM
N_GROUPS = 8
HEADS_PER_GROUP = N_HEADS // N_GROUPS
D_STATE = 128
GROUP_W = D_INNER // N_GROUPS
BC_W = N_GROUPS * D_STATE
SSD_K = 4
XBC = D_INNER + 2 * BC_W
CM_K = 31
CM_HIST = 32
SSD_HIST = 8
N_MEM = 256
XA_HEADS = 4
XA_DIM = D_MODEL // XA_HEADS
CHUNK = 64
EPS = 1e-6
LANES = 128
SUBLANES = 8
SSD_COLS = 2 * D_INNER + 2 * BC_W
VMEM_LIMIT = 56 * 1024 * 1024


def _params(sem):
    return pltpu.CompilerParams(dimension_semantics=sem, vmem_limit_bytes=VMEM_LIMIT)


def _rms(x, g):
    return x * lax.rsqrt(jnp.mean(x * x, axis=-1, keepdims=True) + EPS) * g


def _sigmoid(x):
    return 0.5 * jnp.tanh(0.5 * x) + 0.5


def _silu(x):
    return x * _sigmoid(x)


def _on_row_tiles(i, rows_main, full_fn, split_fn):
    last = pl.num_programs(0) - 1
    pl.when(i != last)(full_fn)
    pl.when(i == last)(lambda: split_fn(rows_main))


def _ffn_body(*refs, final, tail, split, r):
    it = iter(refs)
    x_ref = next(it)
    xt_ref = next(it) if tail else None
    g_ref, wa_ref, wb_ref, wo_ref = next(it), next(it), next(it), next(it)
    fg_ref = next(it) if final else None
    o_ref = next(it)
    ot_ref = next(it) if split else None
    h_ref = next(it)
    acc_ref = o_ref
    i, j = pl.program_id(0), pl.program_id(1)

    def norm_in(x):
        return _rms(x, g_ref[...]).astype(BF16)

    def finish(x, acc):
        y = x + 0.5 * acc
        return _rms(y, fg_ref[...]) if final else y

    @pl.when(j == 0)
    def _():
        acc_ref[...] = jnp.zeros_like(acc_ref)
        if tail:
            def full():
                h_ref[...] = norm_in(x_ref[...])

            def last(rm):
                h_ref[0:rm, :] = norm_in(x_ref[0:rm, :])
                h_ref[rm:, :] = norm_in(xt_ref[...])

            _on_row_tiles(i, r, full, last)
        else:
            h_ref[...] = norm_in(x_ref[...])

    h = h_ref[...]
    a = jnp.dot(h, wa_ref[...].astype(BF16), preferred_element_type=F32)
    b = jnp.dot(h, wb_ref[...].astype(BF16), preferred_element_type=F32)
    acc_ref[...] += jnp.dot((_silu(a) * b).astype(BF16), wo_ref[...].astype(BF16), preferred_element_type=F32)

    @pl.when(j == pl.num_programs(1) - 1)
    def _():
        if tail or split:
            def full():
                o_ref[...] = finish(x_ref[...], acc_ref[...])

            def last(rm):
                o_ref[0:rm, :] = finish(x_ref[0:rm, :], acc_ref[0:rm, :])
                yt = finish(xt_ref[...] if tail else x_ref[rm:, :], acc_ref[rm:, :])
                if split:
                    ot_ref[...] = yt
                else:
                    o_ref[rm:, :] = yt

            _on_row_tiles(i, r, full, last)
        else:
            o_ref[...] = finish(x_ref[...], acc_ref[...])


def _ffn(x, x_tail, g, wi, wo, final_g=None, split_rows=None, *, bm=1040, bf=512):
    d = x.shape[1]
    tail, split, final = x_tail is not None, split_rows is not None, final_g is not None
    m = x.shape[0] + (x_tail.shape[0] if tail else 0)
    n_main = x.shape[0] if tail else (split_rows if split else m)
    nm, nf = m // bm, D_FF // bf
    r = n_main - (nm - 1) * bm
    assert nm * bm == m and 0 < r <= bm and (r == bm or m - n_main == bm - r)
    row = lambda i, j: (i, 0)
    const = lambda i, j: (0, 0)
    in_specs = [pl.BlockSpec((bm, d), row, pipeline_mode=pl.Buffered(1))]
    args = [x]
    if tail:
        in_specs.append(pl.BlockSpec((bm - r, d), const))
        args.append(x_tail)
    in_specs += [
        pl.BlockSpec((1, d), const),
        pl.BlockSpec((d, bf), lambda i, j: (0, j)),
        pl.BlockSpec((d, bf), lambda i, j: (0, j + nf)),
        pl.BlockSpec((bf, d), lambda i, j: (j, 0)),
    ]
    args += [g.reshape(1, d), wi, wi, wo]
    if final:
        in_specs.append(pl.BlockSpec((1, d), const))
        args.append(final_g.reshape(1, d))
    if split:
        out_specs = [pl.BlockSpec((bm, d), row), pl.BlockSpec((bm - r, d), const)]
        out_shape = [jax.ShapeDtypeStruct((n_main, d), F32), jax.ShapeDtypeStruct((m - n_main, d), F32)]
    else:
        out_specs = pl.BlockSpec((bm, d), row)
        out_shape = jax.ShapeDtypeStruct((m, d), F32)
    return pl.pallas_call(
        functools.partial(_ffn_body, final=final, tail=tail, split=split, r=r),
        grid=(nm, nf),
        in_specs=in_specs,
        out_specs=out_specs,
        out_shape=out_shape,
        scratch_shapes=[pltpu.VMEM((bm, d), BF16)],
        compiler_params=_params(("arbitrary", "arbitrary")),
        name="ffn",
    )(*args)


def _norm_mm_body(x_ref, g_ref, w_ref, o_ref, h_ref, *, w_t):
    @pl.when(pl.program_id(1) == 0)
    def _():
        h_ref[...] = _rms(x_ref[...], g_ref[...]).astype(BF16)

    contract = (((1,), (1 if w_t else 0,)), ((), ()))
    o_ref[...] = lax.dot_general(h_ref[...], w_ref[...].astype(BF16), contract,
                                 preferred_element_type=F32).astype(o_ref.dtype)


def _norm_mm(x, g, w, *, bm, bn, w_t=False, n=None, out_dtype=F32, name="norm_mm"):
    m, d = x.shape
    if n is None:
        n = w.shape[0] if w_t else w.shape[1]
    w_spec = pl.BlockSpec((bn, d), lambda i, j: (j, 0)) if w_t else pl.BlockSpec((d, bn), lambda i, j: (0, j))
    return pl.pallas_call(
        functools.partial(_norm_mm_body, w_t=w_t),
        grid=(m // bm, n // bn),
        in_specs=[
            pl.BlockSpec((bm, d), lambda i, j: (i, 0)),
            pl.BlockSpec((1, d), lambda i, j: (0, 0)),
            w_spec,
        ],
        out_specs=pl.BlockSpec((bm, bn), lambda i, j: (i, j)),
        out_shape=jax.ShapeDtypeStruct((m, n), out_dtype),
        scratch_shapes=[pltpu.VMEM((bm, d), BF16)],
        compiler_params=_params(("parallel", "arbitrary")),
        name=name,
    )(x, g.reshape(1, d), w)


def _mm_res_body(a_ref, *rest, tail, r):
    at_ref = rest[0] if tail else None
    w_ref, r_ref, o_ref = rest[-3:]

    def tile(a, res):
        return res + jnp.dot(a, w_ref[...].astype(BF16), preferred_element_type=F32)

    if tail:
        def full():
            o_ref[...] = tile(a_ref[...], r_ref[...])

        def last(rm):
            o_ref[0:rm, :] = tile(a_ref[0:rm, :], r_ref[0:rm, :])
            o_ref[rm:, :] = tile(at_ref[...], r_ref[rm:, :])

        _on_row_tiles(pl.program_id(0), r, full, last)
    else:
        o_ref[...] = tile(a_ref[...], r_ref[...])


def _mm_res(a, a_tail, w, res, *, bm, bn, name):
    k = a.shape[1]
    m, n = res.shape
    tail = a_tail is not None
    nm = m // bm
    r = a.shape[0] - (nm - 1) * bm
    assert nm * bm == m and (not tail or a_tail.shape[0] == bm - r)
    in_specs = [pl.BlockSpec((bm, k), lambda i, j: (i, 0))]
    args = [a]
    if tail:
        in_specs.append(pl.BlockSpec((bm - r, k), lambda i, j: (0, 0)))
        args.append(a_tail)
    in_specs += [pl.BlockSpec((k, bn), lambda i, j: (0, j)), pl.BlockSpec((bm, bn), lambda i, j: (i, j))]
    return pl.pallas_call(
        functools.partial(_mm_res_body, tail=tail, r=r),
        grid=(nm, n // bn),
        in_specs=in_specs,
        out_specs=pl.BlockSpec((bm, bn), lambda i, j: (i, j)),
        out_shape=jax.ShapeDtypeStruct((m, n), F32),
        compiler_params=_params(("parallel", "arbitrary")),
        name=name,
    )(*args, w, res)


def _merge_body(y_ref, yt_ref, u_ref, ut_ref, ws_ref, wc_ref, gs_ref, gc_ref, o_ref, *, r):
    def tile(y, u, gs, gc):
        ssd = jnp.dot(y, ws_ref[...], preferred_element_type=F32)
        cm = jnp.dot(u, wc_ref[...], preferred_element_type=F32)
        return (_sigmoid(gs) * ssd + _sigmoid(gc) * cm).astype(o_ref.dtype)

    def full():
        o_ref[...] = tile(y_ref[...], u_ref[...], gs_ref[...], gc_ref[...])

    def last(rm):
        o_ref[0:rm, :] = tile(y_ref[0:rm, :], u_ref[0:rm, :], gs_ref[0:rm, :], gc_ref[0:rm, :])
        o_ref[rm:, :] = tile(yt_ref[...], ut_ref[...], gs_ref[rm:, :], gc_ref[rm:, :])

    _on_row_tiles(pl.program_id(0), r, full, last)


def _merge(yn, yn_tail, un, un_tail, w_ssd, w_cm, proj, *, bm=640, bn=512):
    m = proj.shape[0]
    nm = m // bm
    r = yn.shape[0] - (nm - 1) * bm
    assert nm * bm == m and yn_tail.shape[0] == bm - r
    gs_off = (2 * D_MODEL) // bn
    gc_off = (3 * D_MODEL) // bn
    return pl.pallas_call(
        functools.partial(_merge_body, r=r),
        grid=(nm, D_MODEL // bn),
        in_specs=[
            pl.BlockSpec((bm, D_INNER), lambda i, j: (i, 0)),
            pl.BlockSpec((bm - r, D_INNER), lambda i, j: (0, 0)),
            pl.BlockSpec((bm, D_MODEL), lambda i, j: (i, 0)),
            pl.BlockSpec((bm - r, D_MODEL), lambda i, j: (0, 0)),
            pl.BlockSpec((D_INNER, bn), lambda i, j: (0, j)),
            pl.BlockSpec((D_MODEL, bn), lambda i, j: (0, j)),
            pl.BlockSpec((bm, bn), lambda i, j: (i, j + gs_off)),
            pl.BlockSpec((bm, bn), lambda i, j: (i, j + gc_off)),
        ],
        out_specs=pl.BlockSpec((bm, bn), lambda i, j: (i, j)),
        out_shape=jax.ShapeDtypeStruct((m, D_MODEL), BF16),
        compiler_params=_params(("parallel", "arbitrary")),
        name="merge",
    )(yn, yn_tail, un, un_tail, w_ssd, w_cm, proj, proj)


def _split3(v):
    hi = v.astype(BF16)
    r = v - hi.astype(F32)
    mid = r.astype(BF16)
    lo = (r - mid.astype(F32)).astype(BF16)
    return hi, mid, lo


def _ssd_body(z_ref, xs_ref, b_ref, c_ref, dt_ref, wx_ref, wb_ref, wc_ref, bx_ref, bb_ref, bc_ref,
              dtb_ref, alog_ref, dexp_ref, ng_ref, e_ref, px_ref, pb_ref, pc_ref, h0_ref,
              y_ref, hout_ref, xhist, bhist, chist, h_ref, *, t):
    @pl.when(pl.program_id(1) == 0)
    def _():
        xhist[...] = px_ref[0]
        bhist[...] = pb_ref[0]
        chist[...] = pc_ref[0]
        for i in range(D_INNER // LANES):
            cs = slice(i * LANES, (i + 1) * LANES)
            h_ref[:, cs] = h0_ref[0, cs, :].T

    def conv_silu(hist, raw_ref, w_ref, bias_ref, cs):
        raw = raw_ref[:, cs]
        win = jnp.concatenate([hist[:, cs], raw], axis=0)
        acc = bias_ref[:, cs] + w_ref[SSD_K - 1:SSD_K, cs] * raw
        for k in range(SSD_K - 1):
            back = SSD_K - 1 - k
            acc = acc + w_ref[k:k + 1, cs] * pltpu.roll(win, t + back, 0)[0:t, :]
        hist[:, cs] = raw[t - SSD_HIST:t, :]
        return _silu(acc)

    every = slice(None)
    bm = conv_silu(bhist, b_ref, wb_ref, bb_ref, every)
    cm = conv_silu(chist, c_ref, wc_ref, bc_ref, every)

    dtp = jax.nn.softplus(dt_ref[...] + dtb_ref[...])
    d_a = dtp * (-jnp.exp(alog_ref[...]))
    row = lax.broadcasted_iota(jnp.int32, (t, LANES), 0)
    a_cs = d_a
    s = 1
    while s < t:
        a_cs = a_cs + jnp.where(row >= s, pltpu.roll(a_cs, s, 0), 0.0)
        s *= 2
    a_last = a_cs[t - 1:t, :]
    e_in = jnp.exp(a_cs)
    e_out = jnp.exp(a_last - a_cs)
    factors = jnp.concatenate(_split3(jnp.concatenate([dtp, e_in, e_out], axis=0)), axis=1)

    pad = jnp.zeros((LANES - t, LANES), F32)
    a_cs_t = jnp.concatenate([a_cs, pad], axis=0).T[:, 0:t]
    li = lax.broadcasted_iota(jnp.int32, (t, t), 0)
    si = lax.broadcasted_iota(jnp.int32, (t, t), 1)
    causal = li >= si

    nt = (((1,), (1,)), ((), ()))
    tn = (((0,), (0,)), ((), ()))
    for g in range(N_GROUPS):
        gs = slice(g * GROUP_W, (g + 1) * GROUP_W)
        ns = slice(g * D_STATE, (g + 1) * D_STATE)
        xs = conv_silu(xhist, xs_ref, wx_ref, bx_ref, gs)
        ex = jnp.dot(factors, e_ref[:, gs], preferred_element_type=F32)
        dt_e, ein_e, eout_e = ex[0:t], ex[t:2 * t], ex[2 * t:3 * t]
        xdt = xs * dt_e
        xdt_b = xdt.astype(BF16)
        xw_b = (xdt * eout_e).astype(BF16)

        c_g = cm[:, ns].astype(BF16)
        b_g = bm[:, ns].astype(BF16)
        cb = lax.dot_general(c_g, b_g, nt, preferred_element_type=F32)
        h_prev = h_ref[:, gs]
        y_off = jnp.dot(c_g, h_prev.astype(BF16), preferred_element_type=F32) * ein_e
        y_heads = []
        for r in range(HEADS_PER_GROUP):
            h = g * HEADS_PER_GROUP + r
            seg = a_cs[:, h:h + 1] - a_cs_t[h:h + 1, :]
            m_h = (cb * jnp.where(causal, jnp.exp(seg), 0.0)).astype(BF16)
            y_heads.append(jnp.dot(m_h, xdt_b[:, r * HEADDIM:(r + 1) * HEADDIM], preferred_element_type=F32))
        y = jnp.concatenate(y_heads, axis=1) + y_off
        states = lax.dot_general(b_g, xw_b, tn, preferred_element_type=F32)
        h_ref[:, gs] = h_prev * ein_e[t - 1:t, :] + states

        y = y + dexp_ref[:, gs] * xs
        v = y * _silu(z_ref[:, gs])
        y_ref[:, gs] = _rms(v, ng_ref[:, gs]).astype(y_ref.dtype)

    @pl.when(pl.program_id(1) == pl.num_programs(1) - 1)
    def _():
        for i in range(D_INNER // LANES):
            cs = slice(i * LANES, (i + 1) * LANES)
            hout_ref[0, cs, :] = h_ref[:, cs].T


def _ssd(proj, pdt, row0, nb, seq, conv_w, conv_b, dt_bias, a_log, d_exp, norm_g, expand, prev, h0):
    t = min(CHUNK, seq)
    nc = seq // t
    rb = row0 // t
    xi, bi, ci = D_INNER // D_INNER, (2 * D_INNER) // BC_W, (2 * D_INNER + BC_W) // BC_W
    rows = lambda b, c: rb + b * nc + c
    full = lambda shape: pl.BlockSpec(shape, lambda b, c: (0, 0))
    in_specs = [
        pl.BlockSpec((t, D_INNER), lambda b, c: (rows(b, c), 0)),
        pl.BlockSpec((t, D_INNER), lambda b, c: (rows(b, c), xi)),
        pl.BlockSpec((t, BC_W), lambda b, c: (rows(b, c), bi)),
        pl.BlockSpec((t, BC_W), lambda b, c: (rows(b, c), ci)),
        pl.BlockSpec((t, LANES), lambda b, c: (rows(b, c), 0)),
        pl.BlockSpec((SSD_K, D_INNER), lambda b, c: (0, 0)),
        pl.BlockSpec((SSD_K, BC_W), lambda b, c: (0, D_INNER // BC_W)),
        pl.BlockSpec((SSD_K, BC_W), lambda b, c: (0, D_INNER // BC_W + 1)),
        pl.BlockSpec((1, D_INNER), lambda b, c: (0, 0)),
        pl.BlockSpec((1, BC_W), lambda b, c: (0, D_INNER // BC_W)),
        pl.BlockSpec((1, BC_W), lambda b, c: (0, D_INNER // BC_W + 1)),
        full((1, LANES)), full((1, LANES)), full((1, D_INNER)), full((1, D_INNER)), full((3 * LANES, D_INNER)),
        pl.BlockSpec((1, SSD_HIST, D_INNER), lambda b, c: (b, 0, 0)),
        pl.BlockSpec((1, SSD_HIST, BC_W), lambda b, c: (b, 0, D_INNER // BC_W)),
        pl.BlockSpec((1, SSD_HIST, BC_W), lambda b, c: (b, 0, D_INNER // BC_W + 1)),
        pl.BlockSpec((1, D_INNER, D_STATE), lambda b, c: (b, 0, 0)),
    ]
    return pl.pallas_call(
        functools.partial(_ssd_body, t=t),
        grid=(nb, nc),
        in_specs=in_specs,
        out_specs=[
            pl.BlockSpec((t, D_INNER), lambda b, c: (b * nc + c, 0)),
            pl.BlockSpec((1, D_INNER, D_STATE), lambda b, c: (b, 0, 0)),
        ],
        out_shape=[
            jax.ShapeDtypeStruct((nb * seq, D_INNER), BF16),
            jax.ShapeDtypeStruct((nb, D_INNER, D_STATE), F32),
        ],
        scratch_shapes=[
            pltpu.VMEM((SSD_HIST, D_INNER), F32),
            pltpu.VMEM((SSD_HIST, BC_W), F32),
            pltpu.VMEM((SSD_HIST, BC_W), F32),
            pltpu.VMEM((D_STATE, D_INNER), F32),
        ],
        compiler_params=_params(("parallel", "arbitrary")),
        name="ssd",
    )(proj, proj, proj, proj, pdt, conv_w, conv_w, conv_w, conv_b, conv_b, conv_b,
      dt_bias, a_log, d_exp, norm_g, expand, prev, prev, prev, h0)


CM_ROWS = 64


def _cm_body(v_ref, g_ref, w_ref, b_ref, lg_ref, lb_ref, p_ref, o_ref, s_ref, buf, ybuf, *, t):
    @pl.when(pl.program_id(1) == 0)
    def _():
        buf[0:CM_HIST, :] = p_ref[0]

    buf[CM_HIST:CM_HIST + t, :] = v_ref[...] * _sigmoid(g_ref[...])
    base = CM_HIST - (CM_K - 1)
    rt = min(CM_ROWS, t)
    win_rows = rt + CM_HIST

    def cols(ci, carry):
        cs = pl.ds(pl.multiple_of(ci * LANES, LANES), LANES)
        for r0 in range(0, t, rt):
            win = buf[r0:r0 + win_rows, cs]
            acc = b_ref[:, cs] + w_ref[CM_K - 1:CM_K, cs] * buf[r0 + CM_HIST:r0 + CM_HIST + rt, cs]
            for s in range(SUBLANES):
                sh = win if s == 0 else pltpu.roll(win, win_rows - s, 0)
                for a in range(CM_HIST // SUBLANES):
                    k = SUBLANES * a + s - base
                    if 0 <= k < CM_K - 1:
                        acc = acc + w_ref[k:k + 1, cs] * sh[SUBLANES * a:SUBLANES * a + rt, :]
            ybuf[r0:r0 + rt, cs] = acc
        return carry

    lax.fori_loop(0, D_MODEL // LANES, cols, 0)

    hist = buf[t:t + CM_HIST, :]
    buf[0:CM_HIST, :] = hist

    @pl.when(pl.program_id(1) == pl.num_programs(1) - 1)
    def _():
        s_ref[0] = hist

    y = ybuf[...]
    mu = jnp.mean(y, axis=-1, keepdims=True)
    yc = y - mu
    var = jnp.mean(yc * yc, axis=-1, keepdims=True)
    o_ref[...] = _silu(yc * lax.rsqrt(var + EPS) * lg_ref[...] + lb_ref[...]).astype(o_ref.dtype)


def _convmod(proj, row0, nb, seq, dw_w, dw_b, ln_g, ln_b, prev):
    t = min(256, seq)
    nc = seq // t
    rb = row0 // t
    vi = 0
    rows = lambda b, c: rb + b * nc + c
    full = lambda shape: pl.BlockSpec(shape, lambda b, c: (0, 0))
    return pl.pallas_call(
        functools.partial(_cm_body, t=t),
        grid=(nb, nc),
        in_specs=[
            pl.BlockSpec((t, D_MODEL), lambda b, c: (rows(b, c), vi)),
            pl.BlockSpec((t, D_MODEL), lambda b, c: (rows(b, c), vi + 1)),
            full((CM_K, D_MODEL)), full((1, D_MODEL)), full((1, D_MODEL)), full((1, D_MODEL)),
            pl.BlockSpec((1, CM_HIST, D_MODEL), lambda b, c: (b, 0, 0)),
        ],
        out_specs=[
            pl.BlockSpec((t, D_MODEL), lambda b, c: (b * nc + c, 0)),
            pl.BlockSpec((1, CM_HIST, D_MODEL), lambda b, c: (b, 0, 0)),
        ],
        out_shape=[
            jax.ShapeDtypeStruct((nb * seq, D_MODEL), BF16),
            jax.ShapeDtypeStruct((nb, CM_HIST, D_MODEL), F32),
        ],
        scratch_shapes=[pltpu.VMEM((CM_HIST + t, D_MODEL), F32), pltpu.VMEM((t, D_MODEL), F32)],
        compiler_params=_params(("parallel", "arbitrary")),
        name="convmod",
    )(proj, proj, dw_w, dw_b, ln_g, ln_b, prev)


def _attn_body(q_ref, k_ref, v_ref, o_ref, kb, vb, *, per_head):
    @pl.when(pl.program_id(1) == 0)
    def _():
        if per_head:
            for h in range(XA_HEADS):
                hs = slice(h * XA_DIM, (h + 1) * XA_DIM)
                kb[:, hs] = k_ref[0, :, h, :].astype(BF16)
                vb[:, hs] = v_ref[0, :, h, :].astype(BF16)
        else:
            kb[...] = k_ref[0].astype(BF16)
            vb[...] = v_ref[0].astype(BF16)

    nt = (((1,), (1,)), ((), ()))
    for h in range(XA_HEADS):
        hs = slice(h * XA_DIM, (h + 1) * XA_DIM)
        s = lax.dot_general(q_ref[:, hs], kb[:, hs], nt, preferred_element_type=F32) * (XA_DIM ** -0.5)
        e = jnp.exp(s - jnp.max(s, axis=-1, keepdims=True))
        p = (e / jnp.sum(e, axis=-1, keepdims=True)).astype(BF16)
        o_ref[:, hs] = jnp.dot(p, vb[:, hs], preferred_element_type=F32).astype(o_ref.dtype)


def _attend(q, row0, nb, seq, k, v):
    tq = min(512, seq)
    nq = seq // tq
    rb = row0 // tq
    per_head = k.ndim == 4
    if per_head:
        kv_spec = pl.BlockSpec((1, N_MEM, XA_HEADS, XA_DIM), lambda b, i: (b, 0, 0, 0))
    else:
        kv_spec = pl.BlockSpec((1, N_MEM, D_MODEL), lambda b, i: (b, 0, 0))
    return pl.pallas_call(
        functools.partial(_attn_body, per_head=per_head),
        grid=(nb, nq),
        in_specs=[pl.BlockSpec((tq, D_MODEL), lambda b, i: (rb + b * nq + i, 0)), kv_spec, kv_spec],
        out_specs=pl.BlockSpec((tq, D_MODEL), lambda b, i: (b * nq + i, 0)),
        out_shape=jax.ShapeDtypeStruct((nb * seq, D_MODEL), BF16),
        scratch_shapes=[pltpu.VMEM((N_MEM, D_MODEL), BF16), pltpu.VMEM((N_MEM, D_MODEL), BF16)],
        compiler_params=_params(("parallel", "arbitrary")),
        name="attend",
    )(q, k, v)


def _pad_rows_front(a, rows):
    return jnp.pad(a, ((0, 0), (rows - a.shape[1], 0), (0, 0)))


def kernel(x_prompt, x_sample, mem_prompt, cache_ssd_conv, cache_ssm_state, cache_cm_conv, cache_mem_k, cache_mem_v, ffn1_norm, ffn1_wi, ffn1_wo, mix_norm, w_in, ssd_conv_w, ssd_conv_b, ssd_dt_bias, ssd_a_log, ssd_d, ssd_norm, ssd_w_out, cm_dw_w, cm_dw_b, cm_ln_g, cm_ln_b, cm_w_out, w_mix_out, xa_norm, mem_norm, xa_wq, xa_wk, xa_wv, xa_wo, ffn2_norm, ffn2_wi, ffn2_wo, final_norm):
    assert ffn1_wi.shape[0] == 1, "single layer"
    bp, lp, _ = x_prompt.shape
    bs, ls, _ = x_sample.shape
    np_rows, ns_rows = bp * lp, bs * ls

    w_t = jnp.transpose(w_in[0])
    w_t_b = w_t.astype(BF16)
    w_dt_t = jnp.pad(w_t[SSD_COLS:SSD_COLS + N_HEADS], ((0, LANES - N_HEADS), (0, 0)))
    w_cm_t = w_t_b[SSD_COLS + N_HEADS:]
    cast = lambda a: a[0].astype(BF16)

    x1 = _ffn(x_prompt.reshape(np_rows, D_MODEL), x_sample.reshape(ns_rows, D_MODEL), ffn1_norm[0],
              cast(ffn1_wi), cast(ffn1_wo))
    proj = _norm_mm(x1, mix_norm[0], w_t_b, bm=1040, bn=1024, w_t=True, n=SSD_COLS, name="in_proj_ssd")
    proj_cm = _norm_mm(x1, mix_norm[0], w_cm_t, bm=1040, bn=1024, w_t=True, name="in_proj_cm")
    pdt = _norm_mm(x1, mix_norm[0], w_dt_t, bm=1040, bn=LANES, w_t=True, name="dt_proj")

    pad_h = lambda a: jnp.pad(a[0].reshape(1, N_HEADS), ((0, 0), (0, LANES - N_HEADS)))
    d_exp = jnp.repeat(ssd_d[0], HEADDIM).reshape(1, D_INNER)
    expand = (lax.broadcasted_iota(jnp.int32, (LANES, D_INNER), 1) // HEADDIM
              == lax.broadcasted_iota(jnp.int32, (LANES, D_INNER), 0)).astype(BF16)
    ssd_args = (ssd_conv_w[0], ssd_conv_b[0].reshape(1, XBC), pad_h(ssd_dt_bias), pad_h(ssd_a_log), d_exp,
                ssd_norm[0].reshape(1, D_INNER), jnp.concatenate([expand] * 3, axis=0))
    yn_p, hp = _ssd(proj, pdt, 0, bp, lp, *ssd_args,
                    jnp.zeros((bp, SSD_HIST, XBC), F32), jnp.zeros((bp, D_INNER, D_STATE), F32))
    yn_s, hs = _ssd(proj, pdt, np_rows, bs, ls, *ssd_args,
                    _pad_rows_front(cache_ssd_conv[0], SSD_HIST), cache_ssm_state[0].reshape(bs, D_INNER, D_STATE))
    xbc_cols = slice(D_INNER, D_INNER + XBC)
    p_ssd_conv = jnp.stack([proj[(b + 1) * lp - (SSD_K - 1):(b + 1) * lp, xbc_cols] for b in range(bp)])
    s_ssd_conv = jnp.concatenate(
        [cache_ssd_conv[0], proj[np_rows:, xbc_cols].reshape(bs, ls, XBC)], axis=1)[:, -(SSD_K - 1):]

    cm_args = (cm_dw_w[0], cm_dw_b[0].reshape(1, D_MODEL), cm_ln_g[0].reshape(1, D_MODEL),
               cm_ln_b[0].reshape(1, D_MODEL))
    un_p, cp = _convmod(proj_cm, 0, bp, lp, *cm_args, jnp.zeros((bp, CM_HIST, D_MODEL), F32))
    un_s, cs = _convmod(proj_cm, np_rows, bs, ls, *cm_args, _pad_rows_front(cache_cm_conv[0], CM_HIST))

    m = _merge(yn_p, yn_s, un_p, un_s, cast(ssd_w_out), cast(cm_w_out), proj_cm)
    x2 = _mm_res(m, None, w_mix_out[0], x1, bm=1040, bn=1024, name="mix_out")

    mem = mem_prompt.reshape(bp * N_MEM, D_MODEL)
    k_p = _norm_mm(mem, mem_norm[0], xa_wk[0], bm=bp * N_MEM, bn=1024, name="mem_k")
    v_p = _norm_mm(mem, mem_norm[0], xa_wv[0], bm=bp * N_MEM, bn=1024, name="mem_v")
    q = _norm_mm(x2, xa_norm[0], xa_wq[0], bm=1040, bn=1024, out_dtype=BF16, name="q_proj")
    o_p = _attend(q, 0, bp, lp, k_p.reshape(bp, N_MEM, D_MODEL), v_p.reshape(bp, N_MEM, D_MODEL))
    o_s = _attend(q, np_rows, bs, ls, cache_mem_k[0], cache_mem_v[0])
    x3 = _mm_res(o_p, o_s, xa_wo[0], x2, bm=1040, bn=1024, name="attn_out")

    y_p, y_s = _ffn(x3, None, ffn2_norm[0], cast(ffn2_wi), cast(ffn2_wo), final_g=final_norm, split_rows=np_rows)

    kv_shape = (1, bp, N_MEM, XA_HEADS, XA_DIM)
    state_shape = lambda b: (1, b, N_HEADS, HEADDIM, D_STATE)
    return (
        y_p.reshape(bp, lp, D_MODEL),
        y_s.reshape(bs, ls, D_MODEL),
        p_ssd_conv[None],
        hp.reshape(state_shape(bp)),
        cp[:, CM_HIST - (CM_K - 1):][None],
        k_p.reshape(kv_shape),
        v_p.reshape(kv_shape),
        s_ssd_conv[None],
        hs.reshape(state_shape(bs)),
        cs[:, CM_HIST - (CM_K - 1):][None],
    )
```

```python
import functools

import jax
import jax.numpy as jnp
from jax import lax
from jax.experimental import pallas as pl
from jax.experimental.pallas import tpu as pltpu

F32 = jnp.float32
BF16 = jnp.bfloat16

D_MODEL = 2048
D_FF = 4 * D_MODEL
D_INNER = 2 * D_MODEL
HEADDIM = 64
N_HEADS = D_INNER // HEADDIM
N_GROUPS = 8
HEADS_PER_GROUP = N_HEADS // N_GROUPS
D_STATE = 128
GROUP_W = D_INNER // N_GROUPS
BC_W = N_GROUPS * D_STATE
SSD_K = 4
XBC = D_INNER + 2 * BC_W
CM_K = 31
CM_HIST = 32
SSD_HIST = 8
N_MEM = 256
XA_HEADS = 4
XA_DIM = D_MODEL // XA_HEADS
CHUNK = 64
EPS = 1e-6
LANES = 128
SUBLANES = 8
SSD_COLS = 2 * D_INNER + 2 * BC_W
VMEM_LIMIT = 56 * 1024 * 1024


def _params(sem):
    return pltpu.CompilerParams(dimension_semantics=sem, vmem_limit_bytes=VMEM_LIMIT)


def _rms(x, g):
    return x * lax.rsqrt(jnp.mean(x * x, axis=-1, keepdims=True) + EPS) * g


def _sigmoid(x):
    return 0.5 * jnp.tanh(0.5 * x) + 0.5


def _silu(x):
    return x * _sigmoid(x)


def _on_row_tiles(i, rows_main, full_fn, split_fn):
    last = pl.num_programs(0) - 1
    pl.when(i != last)(full_fn)
    pl.when(i == last)(lambda: split_fn(rows_main))


def _ffn_body(*refs, final, tail, split, r):
    it = iter(refs)
    x_ref = next(it)
    xt_ref = next(it) if tail else None
    g_ref, wa_ref, wb_ref, wo_ref = next(it), next(it), next(it), next(it)
    fg_ref = next(it) if final else None
    o_ref = next(it)
    ot_ref = next(it) if split else None
    h_ref = next(it)
    acc_ref = o_ref
    i, j = pl.program_id(0), pl.program_id(1)

    def norm_in(x):
        return _rms(x, g_ref[...]).astype(BF16)

    def finish(x, acc):
        y = x + 0.5 * acc
        return _rms(y, fg_ref[...]) if final else y

    @pl.when(j == 0)
    def _():
        acc_ref[...] = jnp.zeros_like(acc_ref)
        if tail:
            def full():
                h_ref[...] = norm_in(x_ref[...])

            def last(rm):
                h_ref[0:rm, :] = norm_in(x_ref[0:rm, :])
                h_ref[rm:, :] = norm_in(xt_ref[...])

            _on_row_tiles(i, r, full, last)
        else:
            h_ref[...] = norm_in(x_ref[...])

    h = h_ref[...]
    a = jnp.dot(h, wa_ref[...].astype(BF16), preferred_element_type=F32)
    b = jnp.dot(h, wb_ref[...].astype(BF16), preferred_element_type=F32)
    acc_ref[...] += jnp.dot((_silu(a) * b).astype(BF16), wo_ref[...].astype(BF16), preferred_element_type=F32)

    @pl.when(j == pl.num_programs(1) - 1)
    def _():
        if tail or split:
            def full():
                o_ref[...] = finish(x_ref[...], acc_ref[...])

            def last(rm):
                o_ref[0:rm, :] = finish(x_ref[0:rm, :], acc_ref[0:rm, :])
                yt = finish(xt_ref[...] if tail else x_ref[rm:, :], acc_ref[rm:, :])
                if split:
                    ot_ref[...] = yt
                else:
                    o_ref[rm:, :] = yt

            _on_row_tiles(i, r, full, last)
        else:
            o_ref[...] = finish(x_ref[...], acc_ref[...])


def _ffn(x, x_tail, g, wi, wo, final_g=None, split_rows=None, *, bm=1040, bf=256):
    d = x.shape[1]
    tail, split, final = x_tail is not None, split_rows is not None, final_g is not None
    m = x.shape[0] + (x_tail.shape[0] if tail else 0)
    n_main = x.shape[0] if tail else (split_rows if split else m)
    nm, nf = m // bm, D_FF // bf
    r = n_main - (nm - 1) * bm
    assert nm * bm == m and 0 < r <= bm and (r == bm or m - n_main == bm - r)
    row = lambda i, j: (i, 0)
    const = lambda i, j: (0, 0)
    in_specs = [pl.BlockSpec((bm, d), row, pipeline_mode=pl.Buffered(1))]
    args = [x]
    if tail:
        in_specs.append(pl.BlockSpec((bm - r, d), const))
        args.append(x_tail)
    in_specs += [
        pl.BlockSpec((1, d), const),
        pl.BlockSpec((d, bf), lambda i, j: (0, j)),
        pl.BlockSpec((d, bf), lambda i, j: (0, j + nf)),
        pl.BlockSpec((bf, d), lambda i, j: (j, 0)),
    ]
    args += [g.reshape(1, d), wi, wi, wo]
    if final:
        in_specs.append(pl.BlockSpec((1, d), const))
        args.append(final_g.reshape(1, d))
    if split:
        out_specs = [pl.BlockSpec((bm, d), row), pl.BlockSpec((bm - r, d), const)]
        out_shape = [jax.ShapeDtypeStruct((n_main, d), F32), jax.ShapeDtypeStruct((m - n_main, d), F32)]
    else:
        out_specs = pl.BlockSpec((bm, d), row)
        out_shape = jax.ShapeDtypeStruct((m, d), F32)
    return pl.pallas_call(
        functools.partial(_ffn_body, final=final, tail=tail, split=split, r=r),
        grid=(nm, nf),
        in_specs=in_specs,
        out_specs=out_specs,
        out_shape=out_shape,
        scratch_shapes=[pltpu.VMEM((bm, d), BF16)],
        compiler_params=_params(("arbitrary", "arbitrary")),
        name="ffn",
    )(*args)


def _norm_mm_body(x_ref, g_ref, w_ref, o_ref, h_ref, *, w_t):
    @pl.when(pl.program_id(1) == 0)
    def _():
        h_ref[...] = _rms(x_ref[...], g_ref[...]).astype(BF16)

    contract = (((1,), (1 if w_t else 0,)), ((), ()))
    o_ref[...] = lax.dot_general(h_ref[...], w_ref[...].astype(BF16), contract,
                                 preferred_element_type=F32).astype(o_ref.dtype)


def _norm_mm(x, g, w, *, bm, bn, w_t=False, n=None, out_dtype=F32, name="norm_mm"):
    m, d = x.shape
    if n is None:
        n = w.shape[0] if w_t else w.shape[1]
    w_spec = pl.BlockSpec((bn, d), lambda i, j: (j, 0)) if w_t else pl.BlockSpec((d, bn), lambda i, j: (0, j))
    return pl.pallas_call(
        functools.partial(_norm_mm_body, w_t=w_t),
        grid=(m // bm, n // bn),
        in_specs=[
            pl.BlockSpec((bm, d), lambda i, j: (i, 0)),
            pl.BlockSpec((1, d), lambda i, j: (0, 0)),
            w_spec,
        ],
        out_specs=pl.BlockSpec((bm, bn), lambda i, j: (i, j)),
        out_shape=jax.ShapeDtypeStruct((m, n), out_dtype),
        scratch_shapes=[pltpu.VMEM((bm, d), BF16)],
        compiler_params=_params(("parallel", "arbitrary")),
        name=name,
    )(x, g.reshape(1, d), w)


def _mm_res_body(a_ref, *rest, tail, r):
    at_ref = rest[0] if tail else None
    w_ref, r_ref, o_ref = rest[-3:]

    def tile(a, res):
        return res + jnp.dot(a, w_ref[...].astype(BF16), preferred_element_type=F32)

    if tail:
        def full():
            o_ref[...] = tile(a_ref[...], r_ref[...])

        def last(rm):
            o_ref[0:rm, :] = tile(a_ref[0:rm, :], r_ref[0:rm, :])
            o_ref[rm:, :] = tile(at_ref[...], r_ref[rm:, :])

        _on_row_tiles(pl.program_id(0), r, full, last)
    else:
        o_ref[...] = tile(a_ref[...], r_ref[...])


def _mm_res(a, a_tail, w, res, *, bm, bn, name):
    k = a.shape[1]
    m, n = res.shape
    tail = a_tail is not None
    nm = m // bm
    r = a.shape[0] - (nm - 1) * bm
    assert nm * bm == m and (not tail or a_tail.shape[0] == bm - r)
    in_specs = [pl.BlockSpec((bm, k), lambda i, j: (i, 0))]
    args = [a]
    if tail:
        in_specs.append(pl.BlockSpec((bm - r, k), lambda i, j: (0, 0)))
        args.append(a_tail)
    in_specs += [pl.BlockSpec((k, bn), lambda i, j: (0, j)), pl.BlockSpec((bm, bn), lambda i, j: (i, j))]
    return pl.pallas_call(
        functools.partial(_mm_res_body, tail=tail, r=r),
        grid=(nm, n // bn),
        in_specs=in_specs,
        out_specs=pl.BlockSpec((bm, bn), lambda i, j: (i, j)),
        out_shape=jax.ShapeDtypeStruct((m, n), F32),
        compiler_params=_params(("parallel", "arbitrary")),
        name=name,
    )(*args, w, res)


def _merge_body(y_ref, yt_ref, u_ref, ut_ref, ws_ref, wc_ref, gs_ref, gc_ref, o_ref, *, r):
    def tile(y, u, gs, gc):
        ssd = jnp.dot(y, ws_ref[...], preferred_element_type=F32)
        cm = jnp.dot(u, wc_ref[...], preferred_element_type=F32)
        return (_sigmoid(gs) * ssd + _sigmoid(gc) * cm).astype(o_ref.dtype)

    def full():
        o_ref[...] = tile(y_ref[...], u_ref[...], gs_ref[...], gc_ref[...])

    def last(rm):
        o_ref[0:rm, :] = tile(y_ref[0:rm, :], u_ref[0:rm, :], gs_ref[0:rm, :], gc_ref[0:rm, :])
        o_ref[rm:, :] = tile(yt_ref[...], ut_ref[...], gs_ref[rm:, :], gc_ref[rm:, :])

    _on_row_tiles(pl.program_id(0), r, full, last)


def _merge(yn, yn_tail, un, un_tail, w_ssd, w_cm, proj, *, bm=640, bn=512):
    m = proj.shape[0]
    nm = m // bm
    r = yn.shape[0] - (nm - 1) * bm
    assert nm * bm == m and yn_tail.shape[0] == bm - r
    gs_off = (2 * D_MODEL) // bn
    gc_off = (3 * D_MODEL) // bn
    return pl.pallas_call(
        functools.partial(_merge_body, r=r),
        grid=(nm, D_MODEL // bn),
        in_specs=[
            pl.BlockSpec((bm, D_INNER), lambda i, j: (i, 0)),
            pl.BlockSpec((bm - r, D_INNER), lambda i, j: (0, 0)),
            pl.BlockSpec((bm, D_MODEL), lambda i, j: (i, 0)),
            pl.BlockSpec((bm - r, D_MODEL), lambda i, j: (0, 0)),
            pl.BlockSpec((D_INNER, bn), lambda i, j: (0, j)),
            pl.BlockSpec((D_MODEL, bn), lambda i, j: (0, j)),
            pl.BlockSpec((bm, bn), lambda i, j: (i, j + gs_off)),
            pl.BlockSpec((bm, bn), lambda i, j: (i, j + gc_off)),
        ],
        out_specs=pl.BlockSpec((bm, bn), lambda i, j: (i, j)),
        out_shape=jax.ShapeDtypeStruct((m, D_MODEL), BF16),
        compiler_params=_params(("parallel", "arbitrary")),
        name="merge",
    )(yn, yn_tail, un, un_tail, w_ssd, w_cm, proj, proj)


def _split3(v):
    hi = v.astype(BF16)
    r = v - hi.astype(F32)
    mid = r.astype(BF16)
    lo = (r - mid.astype(F32)).astype(BF16)
    return hi, mid, lo


def _ssd_body(z_ref, xs_ref, b_ref, c_ref, dt_ref, wx_ref, wb_ref, wc_ref, bx_ref, bb_ref, bc_ref,
              dtb_ref, alog_ref, dexp_ref, ng_ref, e_ref, px_ref, pb_ref, pc_ref, h0_ref,
              y_ref, hout_ref, xhist, bhist, chist, h_ref, *, t, n_sub):
    @pl.when(pl.program_id(1) == 0)
    def _():
        xhist[...] = px_ref[0]
        bhist[...] = pb_ref[0]
        chist[...] = pc_ref[0]
        for i in range(D_INNER // LANES):
            cs = slice(i * LANES, (i + 1) * LANES)
            h_ref[:, cs] = h0_ref[0, cs, :].T

    def chunk(rs):
        def conv_silu(hist, raw_ref, w_ref, bias_ref, cs):
            raw = raw_ref[rs, cs]
            win = jnp.concatenate([hist[:, cs], raw], axis=0)
            acc = bias_ref[:, cs] + w_ref[SSD_K - 1:SSD_K, cs] * raw
            for k in range(SSD_K - 1):
                back = SSD_K - 1 - k
                acc = acc + w_ref[k:k + 1, cs] * pltpu.roll(win, t + back, 0)[0:t, :]
            hist[:, cs] = raw[t - SSD_HIST:t, :]
            return _silu(acc)

        every = slice(None)
        bm = conv_silu(bhist, b_ref, wb_ref, bb_ref, every)
        cm = conv_silu(chist, c_ref, wc_ref, bc_ref, every)

        dtp = jax.nn.softplus(dt_ref[rs, :] + dtb_ref[...])
        d_a = dtp * (-jnp.exp(alog_ref[...]))
        row = lax.broadcasted_iota(jnp.int32, (t, LANES), 0)
        a_cs = d_a
        s = 1
        while s < t:
            a_cs = a_cs + jnp.where(row >= s, pltpu.roll(a_cs, s, 0), 0.0)
            s *= 2
        a_last = a_cs[t - 1:t, :]
        e_in = jnp.exp(a_cs)
        e_out = jnp.exp(a_last - a_cs)
        factors = jnp.concatenate(_split3(jnp.concatenate([dtp, e_in, e_out], axis=0)), axis=1)

        pad = jnp.zeros((LANES - t, LANES), F32)
        a_cs_t = jnp.concatenate([a_cs, pad], axis=0).T[:, 0:t]
        li = lax.broadcasted_iota(jnp.int32, (t, t), 0)
        si = lax.broadcasted_iota(jnp.int32, (t, t), 1)
        causal = li >= si

        nt = (((1,), (1,)), ((), ()))
        tn = (((0,), (0,)), ((), ()))
        for g in range(N_GROUPS):
            gs = slice(g * GROUP_W, (g + 1) * GROUP_W)
            ns = slice(g * D_STATE, (g + 1) * D_STATE)
            xs = conv_silu(xhist, xs_ref, wx_ref, bx_ref, gs)
            ex = jnp.dot(factors, e_ref[:, gs], preferred_element_type=F32)
            dt_e, ein_e, eout_e = ex[0:t], ex[t:2 * t], ex[2 * t:3 * t]
            xdt = xs * dt_e
            xdt_b = xdt.astype(BF16)
            xw_b = (xdt * eout_e).astype(BF16)

            c_g = cm[:, ns].astype(BF16)
            b_g = bm[:, ns].astype(BF16)
            cb = lax.dot_general(c_g, b_g, nt, preferred_element_type=F32)
            h_prev = h_ref[:, gs]
            y_off = jnp.dot(c_g, h_prev.astype(BF16), preferred_element_type=F32) * ein_e
            y_heads = []
            for r in range(HEADS_PER_GROUP):
                h = g * HEADS_PER_GROUP + r
                seg = a_cs[:, h:h + 1] - a_cs_t[h:h + 1, :]
                m_h = (cb * jnp.where(causal, jnp.exp(seg), 0.0)).astype(BF16)
                y_heads.append(jnp.dot(m_h, xdt_b[:, r * HEADDIM:(r + 1) * HEADDIM], preferred_element_type=F32))
            y = jnp.concatenate(y_heads, axis=1) + y_off
            states = lax.dot_general(b_g, xw_b, tn, preferred_element_type=F32)
            h_ref[:, gs] = h_prev * ein_e[t - 1:t, :] + states

            y = y + dexp_ref[:, gs] * xs
            v = y * _silu(z_ref[rs, gs])
            y_ref[rs, gs] = _rms(v, ng_ref[:, gs]).astype(y_ref.dtype)

    for u in range(n_sub):
        chunk(slice(u * t, (u + 1) * t))

    @pl.when(pl.program_id(1) == pl.num_programs(1) - 1)
    def _():
        for i in range(D_INNER // LANES):
            cs = slice(i * LANES, (i + 1) * LANES)
            hout_ref[0, cs, :] = h_ref[:, cs].T


def _ssd(proj, pdt, row0, nb, seq, conv_w, conv_b, dt_bias, a_log, d_exp, norm_g, expand, prev, h0):
    t = min(CHUNK, seq)
    n_sub = 2 if (seq // t) % 2 == 0 else 1
    tb = n_sub * t
    nc = seq // tb
    rb = row0 // tb
    xi, bi, ci = D_INNER // D_INNER, (2 * D_INNER) // BC_W, (2 * D_INNER + BC_W) // BC_W
    rows = lambda b, c: rb + b * nc + c
    full = lambda shape: pl.BlockSpec(shape, lambda b, c: (0, 0))
    in_specs = [
        pl.BlockSpec((tb, D_INNER), lambda b, c: (rows(b, c), 0)),
        pl.BlockSpec((tb, D_INNER), lambda b, c: (rows(b, c), xi)),
        pl.BlockSpec((tb, BC_W), lambda b, c: (rows(b, c), bi)),
        pl.BlockSpec((tb, BC_W), lambda b, c: (rows(b, c), ci)),
        pl.BlockSpec((tb, LANES), lambda b, c: (rows(b, c), 0)),
        pl.BlockSpec((SSD_K, D_INNER), lambda b, c: (0, 0)),
        pl.BlockSpec((SSD_K, BC_W), lambda b, c: (0, D_INNER // BC_W)),
        pl.BlockSpec((SSD_K, BC_W), lambda b, c: (0, D_INNER // BC_W + 1)),
        pl.BlockSpec((1, D_INNER), lambda b, c: (0, 0)),
        pl.BlockSpec((1, BC_W), lambda b, c: (0, D_INNER // BC_W)),
        pl.BlockSpec((1, BC_W), lambda b, c: (0, D_INNER // BC_W + 1)),
        full((1, LANES)), full((1, LANES)), full((1, D_INNER)), full((1, D_INNER)), full((3 * LANES, D_INNER)),
        pl.BlockSpec((1, SSD_HIST, D_INNER), lambda b, c: (b, 0, 0)),
        pl.BlockSpec((1, SSD_HIST, BC_W), lambda b, c: (b, 0, D_INNER // BC_W)),
        pl.BlockSpec((1, SSD_HIST, BC_W), lambda b, c: (b, 0, D_INNER // BC_W + 1)),
        pl.BlockSpec((1, D_INNER, D_STATE), lambda b, c: (b, 0, 0)),
    ]
    return pl.pallas_call(
        functools.partial(_ssd_body, t=t, n_sub=n_sub),
        grid=(nb, nc),
        in_specs=in_specs,
        out_specs=[
            pl.BlockSpec((tb, D_INNER), lambda b, c: (b * nc + c, 0)),
            pl.BlockSpec((1, D_INNER, D_STATE), lambda b, c: (b, 0, 0)),
        ],
        out_shape=[
            jax.ShapeDtypeStruct((nb * seq, D_INNER), BF16),
            jax.ShapeDtypeStruct((nb, D_INNER, D_STATE), F32),
        ],
        scratch_shapes=[
            pltpu.VMEM((SSD_HIST, D_INNER), F32),
            pltpu.VMEM((SSD_HIST, BC_W), F32),
            pltpu.VMEM((SSD_HIST, BC_W), F32),
            pltpu.VMEM((D_STATE, D_INNER), F32),
        ],
        compiler_params=_params(("parallel", "arbitrary")),
        name="ssd",
    )(proj, proj, proj, proj, pdt, conv_w, conv_w, conv_w, conv_b, conv_b, conv_b,
      dt_bias, a_log, d_exp, norm_g, expand, prev, prev, prev, h0)


CM_ROWS = 64


def _cm_body(v_ref, g_ref, w_ref, b_ref, lg_ref, lb_ref, p_ref, o_ref, s_ref, buf, ybuf, *, t):
    @pl.when(pl.program_id(1) == 0)
    def _():
        buf[0:CM_HIST, :] = p_ref[0]

    buf[CM_HIST:CM_HIST + t, :] = v_ref[...] * _sigmoid(g_ref[...])
    base = CM_HIST - (CM_K - 1)
    rt = min(CM_ROWS, t)
    win_rows = rt + CM_HIST

    def cols(ci, carry):
        cs = pl.ds(pl.multiple_of(ci * LANES, LANES), LANES)
        for r0 in range(0, t, rt):
            win = buf[r0:r0 + win_rows, cs]
            acc = b_ref[:, cs] + w_ref[CM_K - 1:CM_K, cs] * buf[r0 + CM_HIST:r0 + CM_HIST + rt, cs]
            for s in range(SUBLANES):
                sh = win if s == 0 else pltpu.roll(win, win_rows - s, 0)
                for a in range(CM_HIST // SUBLANES):
                    k = SUBLANES * a + s - base
                    if 0 <= k < CM_K - 1:
                        acc = acc + w_ref[k:k + 1, cs] * sh[SUBLANES * a:SUBLANES * a + rt, :]
            ybuf[r0:r0 + rt, cs] = acc
        return carry

    lax.fori_loop(0, D_MODEL // LANES, cols, 0)

    hist = buf[t:t + CM_HIST, :]
    buf[0:CM_HIST, :] = hist

    @pl.when(pl.program_id(1) == pl.num_programs(1) - 1)
    def _():
        s_ref[0] = hist

    y = ybuf[...]
    mu = jnp.mean(y, axis=-1, keepdims=True)
    yc = y - mu
    var = jnp.mean(yc * yc, axis=-1, keepdims=True)
    o_ref[...] = _silu(yc * lax.rsqrt(var + EPS) * lg_ref[...] + lb_ref[...]).astype(o_ref.dtype)


def _convmod(proj, row0, nb, seq, dw_w, dw_b, ln_g, ln_b, prev):
    t = min(256, seq)
    nc = seq // t
    rb = row0 // t
    vi = 0
    rows = lambda b, c: rb + b * nc + c
    full = lambda shape: pl.BlockSpec(shape, lambda b, c: (0, 0))
    return pl.pallas_call(
        functools.partial(_cm_body, t=t),
        grid=(nb, nc),
        in_specs=[
            pl.BlockSpec((t, D_MODEL), lambda b, c: (rows(b, c), vi)),
            pl.BlockSpec((t, D_MODEL), lambda b, c: (rows(b, c), vi + 1)),
            full((CM_K, D_MODEL)), full((1, D_MODEL)), full((1, D_MODEL)), full((1, D_MODEL)),
            pl.BlockSpec((1, CM_HIST, D_MODEL), lambda b, c: (b, 0, 0)),
        ],
        out_specs=[
            pl.BlockSpec((t, D_MODEL), lambda b, c: (b * nc + c, 0)),
            pl.BlockSpec((1, CM_HIST, D_MODEL), lambda b, c: (b, 0, 0)),
        ],
        out_shape=[
            jax.ShapeDtypeStruct((nb * seq, D_MODEL), BF16),
            jax.ShapeDtypeStruct((nb, CM_HIST, D_MODEL), F32),
        ],
        scratch_shapes=[pltpu.VMEM((CM_HIST + t, D_MODEL), F32), pltpu.VMEM((t, D_MODEL), F32)],
        compiler_params=_params(("parallel", "arbitrary")),
        name="convmod",
    )(proj, proj, dw_w, dw_b, ln_g, ln_b, prev)


def _attn_body(q_ref, k_ref, v_ref, o_ref, kb, vb, *, per_head):
    @pl.when(pl.program_id(1) == 0)
    def _():
        if per_head:
            for h in range(XA_HEADS):
                hs = slice(h * XA_DIM, (h + 1) * XA_DIM)
                kb[:, hs] = k_ref[0, :, h, :].astype(BF16)
                vb[:, hs] = v_ref[0, :, h, :].astype(BF16)
        else:
            kb[...] = k_ref[0].astype(BF16)
            vb[...] = v_ref[0].astype(BF16)

    nt = (((1,), (1,)), ((), ()))
    for h in range(XA_HEADS):
        hs = slice(h * XA_DIM, (h + 1) * XA_DIM)
        s = lax.dot_general(q_ref[:, hs], kb[:, hs], nt, preferred_element_type=F32) * (XA_DIM ** -0.5)
        e = jnp.exp(s - jnp.max(s, axis=-1, keepdims=True))
        p = (e / jnp.sum(e, axis=-1, keepdims=True)).astype(BF16)
        o_ref[:, hs] = jnp.dot(p, vb[:, hs], preferred_element_type=F32).astype(o_ref.dtype)


def _attend(q, row0, nb, seq, k, v):
    tq = min(512, seq)
    nq = seq // tq
    rb = row0 // tq
    per_head = k.ndim == 4
    if per_head:
        kv_spec = pl.BlockSpec((1, N_MEM, XA_HEADS, XA_DIM), lambda b, i: (b, 0, 0, 0))
    else:
        kv_spec = pl.BlockSpec((1, N_MEM, D_MODEL), lambda b, i: (b, 0, 0))
    return pl.pallas_call(
        functools.partial(_attn_body, per_head=per_head),
        grid=(nb, nq),
        in_specs=[pl.BlockSpec((tq, D_MODEL), lambda b, i: (rb + b * nq + i, 0)), kv_spec, kv_spec],
        out_specs=pl.BlockSpec((tq, D_MODEL), lambda b, i: (b * nq + i, 0)),
        out_shape=jax.ShapeDtypeStruct((nb * seq, D_MODEL), BF16),
        scratch_shapes=[pltpu.VMEM((N_MEM, D_MODEL), BF16), pltpu.VMEM((N_MEM, D_MODEL), BF16)],
        compiler_params=_params(("parallel", "arbitrary")),
        name="attend",
    )(q, k, v)


def _pad_rows_front(a, rows):
    return jnp.pad(a, ((0, 0), (rows - a.shape[1], 0), (0, 0)))


def kernel(x_prompt, x_sample, mem_prompt, cache_ssd_conv, cache_ssm_state, cache_cm_conv, cache_mem_k, cache_mem_v, ffn1_norm, ffn1_wi, ffn1_wo, mix_norm, w_in, ssd_conv_w, ssd_conv_b, ssd_dt_bias, ssd_a_log, ssd_d, ssd_norm, ssd_w_out, cm_dw_w, cm_dw_b, cm_ln_g, cm_ln_b, cm_w_out, w_mix_out, xa_norm, mem_norm, xa_wq, xa_wk, xa_wv, xa_wo, ffn2_norm, ffn2_wi, ffn2_wo, final_norm):
    assert ffn1_wi.shape[0] == 1, "single layer"
    bp, lp, _ = x_prompt.shape
    bs, ls, _ = x_sample.shape
    np_rows, ns_rows = bp * lp, bs * ls

    w_t = jnp.transpose(w_in[0])
    w_t_b = w_t.astype(BF16)
    w_dt_t = jnp.pad(w_t[SSD_COLS:SSD_COLS + N_HEADS], ((0, LANES - N_HEADS), (0, 0)))
    w_cm_t = w_t_b[SSD_COLS + N_HEADS:]
    cast = lambda a: a[0].astype(BF16)

    x1 = _ffn(x_prompt.reshape(np_rows, D_MODEL), x_sample.reshape(ns_rows, D_MODEL), ffn1_norm[0],
              ffn1_wi[0], ffn1_wo[0])
    proj = _norm_mm(x1, mix_norm[0], w_t_b, bm=1040, bn=1024, w_t=True, n=SSD_COLS, name="in_proj_ssd")
    proj_cm = _norm_mm(x1, mix_norm[0], w_cm_t, bm=1040, bn=1024, w_t=True, name="in_proj_cm")
    pdt = _norm_mm(x1, mix_norm[0], w_dt_t, bm=1040, bn=LANES, w_t=True, name="dt_proj")

    pad_h = lambda a: jnp.pad(a[0].reshape(1, N_HEADS), ((0, 0), (0, LANES - N_HEADS)))
    d_exp = jnp.repeat(ssd_d[0], HEADDIM).reshape(1, D_INNER)
    expand = (lax.broadcasted_iota(jnp.int32, (LANES, D_INNER), 1) // HEADDIM
              == lax.broadcasted_iota(jnp.int32, (LANES, D_INNER), 0)).astype(BF16)
    ssd_args = (ssd_conv_w[0], ssd_conv_b[0].reshape(1, XBC), pad_h(ssd_dt_bias), pad_h(ssd_a_log), d_exp,
                ssd_norm[0].reshape(1, D_INNER), jnp.concatenate([expand] * 3, axis=0))
    yn_p, hp = _ssd(proj, pdt, 0, bp, lp, *ssd_args,
                    jnp.zeros((bp, SSD_HIST, XBC), F32), jnp.zeros((bp, D_INNER, D_STATE), F32))
    yn_s, hs = _ssd(proj, pdt, np_rows, bs, ls, *ssd_args,
                    _pad_rows_front(cache_ssd_conv[0], SSD_HIST), cache_ssm_state[0].reshape(bs, D_INNER, D_STATE))
    xbc_cols = slice(D_INNER, D_INNER + XBC)
    p_ssd_conv = jnp.stack([proj[(b + 1) * lp - (SSD_K - 1):(b + 1) * lp, xbc_cols] for b in range(bp)])
    s_ssd_conv = jnp.concatenate(
        [cache_ssd_conv[0], proj[np_rows:, xbc_cols].reshape(bs, ls, XBC)], axis=1)[:, -(SSD_K - 1):]

    cm_args = (cm_dw_w[0], cm_dw_b[0].reshape(1, D_MODEL), cm_ln_g[0].reshape(1, D_MODEL),
               cm_ln_b[0].reshape(1, D_MODEL))
    un_p, cp = _convmod(proj_cm, 0, bp, lp, *cm_args, jnp.zeros((bp, CM_HIST, D_MODEL), F32))
    un_s, cs = _convmod(proj_cm, np_rows, bs, ls, *cm_args, _pad_rows_front(cache_cm_conv[0], CM_HIST))

    m = _merge(yn_p, yn_s, un_p, un_s, cast(ssd_w_out), cast(cm_w_out), proj_cm)
    x2 = _mm_res(m, None, w_mix_out[0], x1, bm=1040, bn=1024, name="mix_out")

    mem = mem_prompt.reshape(bp * N_MEM, D_MODEL)
    k_p = _norm_mm(mem, mem_norm[0], xa_wk[0], bm=bp * N_MEM, bn=1024, name="mem_k")
    v_p = _norm_mm(mem, mem_norm[0], xa_wv[0], bm=bp * N_MEM, bn=1024, name="mem_v")
    q = _norm_mm(x2, xa_norm[0], xa_wq[0], bm=1040, bn=1024, out_dtype=BF16, name="q_proj")
    o_p = _attend(q, 0, bp, lp, k_p.reshape(bp, N_MEM, D_MODEL), v_p.reshape(bp, N_MEM, D_MODEL))
    o_s = _attend(q, np_rows, bs, ls, cache_mem_k[0], cache_mem_v[0])
    x3 = _mm_res(o_p, o_s, xa_wo[0], x2, bm=1040, bn=1024, name="attn_out")

    y_p, y_s = _ffn(x3, None, ffn2_norm[0], ffn2_wi[0], ffn2_wo[0], final_g=final_norm, split_rows=np_rows)

    kv_shape = (1, bp, N_MEM, XA_HEADS, XA_DIM)
    state_shape = lambda b: (1, b, N_HEADS, HEADDIM, D_STATE)
    return (
        y_p.reshape(bp, lp, D_MODEL),
        y_s.reshape(bs, ls, D_MODEL),
        p_ssd_conv[None],
        hp.reshape(state_shape(bp)),
        cp[:, CM_HIST - (CM_K - 1):][None],
        k_p.reshape(kv_shape),
        v_p.reshape(kv_shape),
        s_ssd_conv[None],
        hs.reshape(state_shape(bs)),
        cs[:, CM_HIST - (CM_K - 1):][None],
    )
```

```python
import functools

import jax
import jax.numpy as jnp
from jax import lax
from jax.experimental import pallas as pl
from jax.experimental.pallas import tpu as pltpu

F32 = jnp.float32
BF16 = jnp.bfloat16

D_MODEL = 2048
D_FF = 4 * D_MODEL
D_INNER = 2 * D_MODEL
HEADDIM = 64
N_HEADS = D_INNER // HEADDIM
N_GROUPS = 8
HEADS_PER_GROUP = N_HEADS // N_GROUPS
D_STATE = 128
GROUP_W = D_INNER // N_GROUPS
BC_W = N_GROUPS * D_STATE
SSD_K = 4
XBC = D_INNER + 2 * BC_W
CM_K = 31
CM_HIST = 32
SSD_HIST = 8
N_MEM = 256
XA_HEADS = 4
XA_DIM = D_MODEL // XA_HEADS
CHUNK = 64
EPS = 1e-6
LANES = 128
SUBLANES = 8
SSD_COLS = 2 * D_INNER + 2 * BC_W
VMEM_LIMIT = 56 * 1024 * 1024


def _params(sem):
    return pltpu.CompilerParams(dimension_semantics=sem, vmem_limit_bytes=VMEM_LIMIT)


def _rms(x, g):
    return x * lax.rsqrt(jnp.mean(x * x, axis=-1, keepdims=True) + EPS) * g


def _sigmoid(x):
    return 0.5 * jnp.tanh(0.5 * x) + 0.5


def _silu(x):
    return x * _sigmoid(x)


def _on_row_tiles(i, rows_main, full_fn, split_fn):
    last = pl.num_programs(0) - 1
    pl.when(i != last)(full_fn)
    pl.when(i == last)(lambda: split_fn(rows_main))


def _ffn_body(*refs, final, tail, split, r):
    it = iter(refs)
    x_ref = next(it)
    xt_ref = next(it) if tail else None
    g_ref, wa_ref, wb_ref, wo_ref = next(it), next(it), next(it), next(it)
    fg_ref = next(it) if final else None
    o_ref = next(it)
    ot_ref = next(it) if split else None
    h_ref = next(it)
    acc_ref = o_ref
    i, j = pl.program_id(0), pl.program_id(1)

    def norm_in(x):
        return _rms(x, g_ref[...]).astype(BF16)

    def finish(x, acc):
        y = x + 0.5 * acc
        return _rms(y, fg_ref[...]) if final else y

    @pl.when(j == 0)
    def _():
        acc_ref[...] = jnp.zeros_like(acc_ref)
        if tail:
            def full():
                h_ref[...] = norm_in(x_ref[...])

            def last(rm):
                h_ref[0:rm, :] = norm_in(x_ref[0:rm, :])
                h_ref[rm:, :] = norm_in(xt_ref[...])

            _on_row_tiles(i, r, full, last)
        else:
            h_ref[...] = norm_in(x_ref[...])

    h = h_ref[...]
    a = jnp.dot(h, wa_ref[...].astype(BF16), preferred_element_type=F32)
    b = jnp.dot(h, wb_ref[...].astype(BF16), preferred_element_type=F32)
    acc_ref[...] += jnp.dot((_silu(a) * b).astype(BF16), wo_ref[...].astype(BF16), preferred_element_type=F32)

    @pl.when(j == pl.num_programs(1) - 1)
    def _():
        if tail or split:
            def full():
                o_ref[...] = finish(x_ref[...], acc_ref[...])

            def last(rm):
                o_ref[0:rm, :] = finish(x_ref[0:rm, :], acc_ref[0:rm, :])
                yt = finish(xt_ref[...] if tail else x_ref[rm:, :], acc_ref[rm:, :])
                if split:
                    ot_ref[...] = yt
                else:
                    o_ref[rm:, :] = yt

            _on_row_tiles(i, r, full, last)
        else:
            o_ref[...] = finish(x_ref[...], acc_ref[...])


def _ffn(x, x_tail, g, wi, wo, final_g=None, split_rows=None, *, bm=1040, bf=256):
    d = x.shape[1]
    tail, split, final = x_tail is not None, split_rows is not None, final_g is not None
    m = x.shape[0] + (x_tail.shape[0] if tail else 0)
    n_main = x.shape[0] if tail else (split_rows if split else m)
    nm, nf = m // bm, D_FF // bf
    r = n_main - (nm - 1) * bm
    assert nm * bm == m and 0 < r <= bm and (r == bm or m - n_main == bm - r)
    row = lambda i, j: (i, 0)
    const = lambda i, j: (0, 0)
    in_specs = [pl.BlockSpec((bm, d), row, pipeline_mode=pl.Buffered(1))]
    args = [x]
    if tail:
        in_specs.append(pl.BlockSpec((bm - r, d), const))
        args.append(x_tail)
    in_specs += [
        pl.BlockSpec((1, d), const),
        pl.BlockSpec((d, bf), lambda i, j: (0, j)),
        pl.BlockSpec((d, bf), lambda i, j: (0, j + nf)),
        pl.BlockSpec((bf, d), lambda i, j: (j, 0)),
    ]
    args += [g.reshape(1, d), wi, wi, wo]
    if final:
        in_specs.append(pl.BlockSpec((1, d), const))
        args.append(final_g.reshape(1, d))
    if split:
        out_specs = [pl.BlockSpec((bm, d), row), pl.BlockSpec((bm - r, d), const)]
        out_shape = [jax.ShapeDtypeStruct((n_main, d), F32), jax.ShapeDtypeStruct((m - n_main, d), F32)]
    else:
        out_specs = pl.BlockSpec((bm, d), row)
        out_shape = jax.ShapeDtypeStruct((m, d), F32)
    return pl.pallas_call(
        functools.partial(_ffn_body, final=final, tail=tail, split=split, r=r),
        grid=(nm, nf),
        in_specs=in_specs,
        out_specs=out_specs,
        out_shape=out_shape,
        scratch_shapes=[pltpu.VMEM((bm, d), BF16)],
        compiler_params=_params(("arbitrary", "arbitrary")),
        name="ffn",
    )(*args)


def _norm_mm_body(x_ref, g_ref, w_ref, o_ref, h_ref, *, w_t):
    @pl.when(pl.program_id(1) == 0)
    def _():
        h_ref[...] = _rms(x_ref[...], g_ref[...]).astype(BF16)

    contract = (((1,), (1 if w_t else 0,)), ((), ()))
    o_ref[...] = lax.dot_general(h_ref[...], w_ref[...].astype(BF16), contract,
                                 preferred_element_type=F32).astype(o_ref.dtype)


def _norm_mm(x, g, w, *, bm, bn, w_t=False, n=None, out_dtype=F32, name="norm_mm"):
    m, d = x.shape
    if n is None:
        n = w.shape[0] if w_t else w.shape[1]
    w_spec = pl.BlockSpec((bn, d), lambda i, j: (j, 0)) if w_t else pl.BlockSpec((d, bn), lambda i, j: (0, j))
    return pl.pallas_call(
        functools.partial(_norm_mm_body, w_t=w_t),
        grid=(m // bm, n // bn),
        in_specs=[
            pl.BlockSpec((bm, d), lambda i, j: (i, 0)),
            pl.BlockSpec((1, d), lambda i, j: (0, 0)),
            w_spec,
        ],
        out_specs=pl.BlockSpec((bm, bn), lambda i, j: (i, j)),
        out_shape=jax.ShapeDtypeStruct((m, n), out_dtype),
        scratch_shapes=[pltpu.VMEM((bm, d), BF16)],
        compiler_params=_params(("parallel", "arbitrary")),
        name=name,
    )(x, g.reshape(1, d), w)


def _mm_res_body(a_ref, *rest, tail, r):
    at_ref = rest[0] if tail else None
    w_ref, r_ref, o_ref = rest[-3:]

    def tile(a, res):
        return res + jnp.dot(a, w_ref[...].astype(BF16), preferred_element_type=F32)

    if tail:
        def full():
            o_ref[...] = tile(a_ref[...], r_ref[...])

        def last(rm):
            o_ref[0:rm, :] = tile(a_ref[0:rm, :], r_ref[0:rm, :])
            o_ref[rm:, :] = tile(at_ref[...], r_ref[rm:, :])

        _on_row_tiles(pl.program_id(0), r, full, last)
    else:
        o_ref[...] = tile(a_ref[...], r_ref[...])


def _mm_res(a, a_tail, w, res, *, bm, bn, name):
    k = a.shape[1]
    m, n = res.shape
    tail = a_tail is not None
    nm = m // bm
    r = a.shape[0] - (nm - 1) * bm
    assert nm * bm == m and (not tail or a_tail.shape[0] == bm - r)
    in_specs = [pl.BlockSpec((bm, k), lambda i, j: (i, 0))]
    args = [a]
    if tail:
        in_specs.append(pl.BlockSpec((bm - r, k), lambda i, j: (0, 0)))
        args.append(a_tail)
    in_specs += [pl.BlockSpec((k, bn), lambda i, j: (0, j)), pl.BlockSpec((bm, bn), lambda i, j: (i, j))]
    return pl.pallas_call(
        functools.partial(_mm_res_body, tail=tail, r=r),
        grid=(nm, n // bn),
        in_specs=in_specs,
        out_specs=pl.BlockSpec((bm, bn), lambda i, j: (i, j)),
        out_shape=jax.ShapeDtypeStruct((m, n), F32),
        compiler_params=_params(("parallel", "arbitrary")),
        name=name,
    )(*args, w, res)


def _merge_body(y_ref, yt_ref, u_ref, ut_ref, ws_ref, wc_ref, gs_ref, gc_ref, o_ref, *, r):
    def tile(y, u, gs, gc):
        ssd = jnp.dot(y, ws_ref[...], preferred_element_type=F32)
        cm = jnp.dot(u, wc_ref[...], preferred_element_type=F32)
        return (_sigmoid(gs) * ssd + _sigmoid(gc) * cm).astype(o_ref.dtype)

    def full():
        o_ref[...] = tile(y_ref[...], u_ref[...], gs_ref[...], gc_ref[...])

    def last(rm):
        o_ref[0:rm, :] = tile(y_ref[0:rm, :], u_ref[0:rm, :], gs_ref[0:rm, :], gc_ref[0:rm, :])
        o_ref[rm:, :] = tile(yt_ref[...], ut_ref[...], gs_ref[rm:, :], gc_ref[rm:, :])

    _on_row_tiles(pl.program_id(0), r, full, last)


def _merge(yn, yn_tail, un, un_tail, w_ssd, w_cm, proj, *, bm=640, bn=512):
    m = proj.shape[0]
    nm = m // bm
    r = yn.shape[0] - (nm - 1) * bm
    assert nm * bm == m and yn_tail.shape[0] == bm - r
    gs_off = (2 * D_MODEL) // bn
    gc_off = (3 * D_MODEL) // bn
    return pl.pallas_call(
        functools.partial(_merge_body, r=r),
        grid=(nm, D_MODEL // bn),
        in_specs=[
            pl.BlockSpec((bm, D_INNER), lambda i, j: (i, 0)),
            pl.BlockSpec((bm - r, D_INNER), lambda i, j: (0, 0)),
            pl.BlockSpec((bm, D_MODEL), lambda i, j: (i, 0)),
            pl.BlockSpec((bm - r, D_MODEL), lambda i, j: (0, 0)),
            pl.BlockSpec((D_INNER, bn), lambda i, j: (0, j)),
            pl.BlockSpec((D_MODEL, bn), lambda i, j: (0, j)),
            pl.BlockSpec((bm, bn), lambda i, j: (i, j + gs_off)),
            pl.BlockSpec((bm, bn), lambda i, j: (i, j + gc_off)),
        ],
        out_specs=pl.BlockSpec((bm, bn), lambda i, j: (i, j)),
        out_shape=jax.ShapeDtypeStruct((m, D_MODEL), BF16),
        compiler_params=_params(("parallel", "arbitrary")),
        name="merge",
    )(yn, yn_tail, un, un_tail, w_ssd, w_cm, proj, proj)


def _split3(v):
    hi = v.astype(BF16)
    r = v - hi.astype(F32)
    mid = r.astype(BF16)
    lo = (r - mid.astype(F32)).astype(BF16)
    return hi, mid, lo


def _ssd_body(z_ref, xs_ref, b_ref, c_ref, dt_ref, wx_ref, wb_ref, wc_ref, bx_ref, bb_ref, bc_ref,
              dtb_ref, alog_ref, dexp_ref, ng_ref, e_ref, px_ref, pb_ref, pc_ref, h0_ref,
              y_ref, hout_ref, xhist, bhist, chist, h_ref, *, t, n_sub):
    @pl.when(pl.program_id(1) == 0)
    def _():
        xhist[...] = px_ref[0]
        bhist[...] = pb_ref[0]
        chist[...] = pc_ref[0]
        for i in range(D_INNER // LANES):
            cs = slice(i * LANES, (i + 1) * LANES)
            h_ref[:, cs] = h0_ref[0, cs, :].T

    def chunk(rs):
        def conv_silu(hist, raw_ref, w_ref, bias_ref, cs):
            raw = raw_ref[rs, cs]
            win = jnp.concatenate([hist[:, cs], raw], axis=0)
            acc = bias_ref[:, cs] + w_ref[SSD_K - 1:SSD_K, cs] * raw
            for k in range(SSD_K - 1):
                back = SSD_K - 1 - k
                acc = acc + w_ref[k:k + 1, cs] * pltpu.roll(win, t + back, 0)[0:t, :]
            hist[:, cs] = raw[t - SSD_HIST:t, :]
            return _silu(acc)

        every = slice(None)
        bm = conv_silu(bhist, b_ref, wb_ref, bb_ref, every)
        cm = conv_silu(chist, c_ref, wc_ref, bc_ref, every)

        dtp = jax.nn.softplus(dt_ref[rs, :] + dtb_ref[...])
        d_a = dtp * (-jnp.exp(alog_ref[...]))
        row = lax.broadcasted_iota(jnp.int32, (t, LANES), 0)
        a_cs = d_a
        s = 1
        while s < t:
            a_cs = a_cs + jnp.where(row >= s, pltpu.roll(a_cs, s, 0), 0.0)
            s *= 2
        a_last = a_cs[t - 1:t, :]
        e_in = jnp.exp(a_cs)
        e_out = jnp.exp(a_last - a_cs)
        factors = jnp.concatenate(_split3(jnp.concatenate([dtp, e_in, e_out], axis=0)), axis=1)

        pad = jnp.zeros((LANES - t, LANES), F32)
        a_cs_t = jnp.concatenate([a_cs, pad], axis=0).T[:, 0:t]
        li = lax.broadcasted_iota(jnp.int32, (t, t), 0)
        si = lax.broadcasted_iota(jnp.int32, (t, t), 1)
        causal = li >= si

        nt = (((1,), (1,)), ((), ()))
        tn = (((0,), (0,)), ((), ()))
        for g in range(N_GROUPS):
            gs = slice(g * GROUP_W, (g + 1) * GROUP_W)
            ns = slice(g * D_STATE, (g + 1) * D_STATE)
            xs = conv_silu(xhist, xs_ref, wx_ref, bx_ref, gs)
            ex = jnp.dot(factors, e_ref[:, gs], preferred_element_type=F32)
            dt_e, ein_e, eout_e = ex[0:t], ex[t:2 * t], ex[2 * t:3 * t]
            xdt = xs * dt_e
            xdt_b = xdt.astype(BF16)
            xw_b = (xdt * eout_e).astype(BF16)

            c_g = cm[:, ns].astype(BF16)
            b_g = bm[:, ns].astype(BF16)
            cb = lax.dot_general(c_g, b_g, nt, preferred_element_type=F32)
            h_prev = h_ref[:, gs]
            y_off = jnp.dot(c_g, h_prev.astype(BF16), preferred_element_type=F32) * ein_e
            y_heads = []
            for r in range(HEADS_PER_GROUP):
                h = g * HEADS_PER_GROUP + r
                seg = a_cs[:, h:h + 1] - a_cs_t[h:h + 1, :]
                m_h = (cb * jnp.where(causal, jnp.exp(seg), 0.0)).astype(BF16)
                y_heads.append(jnp.dot(m_h, xdt_b[:, r * HEADDIM:(r + 1) * HEADDIM], preferred_element_type=F32))
            y = jnp.concatenate(y_heads, axis=1) + y_off
            states = lax.dot_general(b_g, xw_b, tn, preferred_element_type=F32)
            h_ref[:, gs] = h_prev * ein_e[t - 1:t, :] + states

            y = y + dexp_ref[:, gs] * xs
            v = y * _silu(z_ref[rs, gs])
            y_ref[rs, gs] = _rms(v, ng_ref[:, gs]).astype(y_ref.dtype)

    for u in range(n_sub):
        chunk(slice(u * t, (u + 1) * t))

    @pl.when(pl.program_id(1) == pl.num_programs(1) - 1)
    def _():
        for i in range(D_INNER // LANES):
            cs = slice(i * LANES, (i + 1) * LANES)
            hout_ref[0, cs, :] = h_ref[:, cs].T


def _ssd(proj, pdt, row0, nb, seq, conv_w, conv_b, dt_bias, a_log, d_exp, norm_g, expand, prev, h0):
    t = min(CHUNK, seq)
    n_sub = next(n for n in (4, 2, 1) if (seq // t) % n == 0)
    tb = n_sub * t
    nc = seq // tb
    rb = row0 // tb
    xi, bi, ci = D_INNER // D_INNER, (2 * D_INNER) // BC_W, (2 * D_INNER + BC_W) // BC_W
    rows = lambda b, c: rb + b * nc + c
    full = lambda shape: pl.BlockSpec(shape, lambda b, c: (0, 0))
    in_specs = [
        pl.BlockSpec((tb, D_INNER), lambda b, c: (rows(b, c), 0)),
        pl.BlockSpec((tb, D_INNER), lambda b, c: (rows(b, c), xi)),
        pl.BlockSpec((tb, BC_W), lambda b, c: (rows(b, c), bi)),
        pl.BlockSpec((tb, BC_W), lambda b, c: (rows(b, c), ci)),
        pl.BlockSpec((tb, LANES), lambda b, c: (rows(b, c), 0)),
        pl.BlockSpec((SSD_K, D_INNER), lambda b, c: (0, 0)),
        pl.BlockSpec((SSD_K, BC_W), lambda b, c: (0, D_INNER // BC_W)),
        pl.BlockSpec((SSD_K, BC_W), lambda b, c: (0, D_INNER // BC_W + 1)),
        pl.BlockSpec((1, D_INNER), lambda b, c: (0, 0)),
        pl.BlockSpec((1, BC_W), lambda b, c: (0, D_INNER // BC_W)),
        pl.BlockSpec((1, BC_W), lambda b, c: (0, D_INNER // BC_W + 1)),
        full((1, LANES)), full((1, LANES)), full((1, D_INNER)), full((1, D_INNER)), full((3 * LANES, D_INNER)),
        pl.BlockSpec((1, SSD_HIST, D_INNER), lambda b, c: (b, 0, 0)),
        pl.BlockSpec((1, SSD_HIST, BC_W), lambda b, c: (b, 0, D_INNER // BC_W)),
        pl.BlockSpec((1, SSD_HIST, BC_W), lambda b, c: (b, 0, D_INNER // BC_W + 1)),
        pl.BlockSpec((1, D_INNER, D_STATE), lambda b, c: (b, 0, 0)),
    ]
    return pl.pallas_call(
        functools.partial(_ssd_body, t=t, n_sub=n_sub),
        grid=(nb, nc),
        in_specs=in_specs,
        out_specs=[
            pl.BlockSpec((tb, D_INNER), lambda b, c: (b * nc + c, 0)),
            pl.BlockSpec((1, D_INNER, D_STATE), lambda b, c: (b, 0, 0)),
        ],
        out_shape=[
            jax.ShapeDtypeStruct((nb * seq, D_INNER), BF16),
            jax.ShapeDtypeStruct((nb, D_INNER, D_STATE), F32),
        ],
        scratch_shapes=[
            pltpu.VMEM((SSD_HIST, D_INNER), F32),
            pltpu.VMEM((SSD_HIST, BC_W), F32),
            pltpu.VMEM((SSD_HIST, BC_W), F32),
            pltpu.VMEM((D_STATE, D_INNER), F32),
        ],
        compiler_params=_params(("parallel", "arbitrary")),
        name="ssd",
    )(proj, proj, proj, proj, pdt, conv_w, conv_w, conv_w, conv_b, conv_b, conv_b,
      dt_bias, a_log, d_exp, norm_g, expand, prev, prev, prev, h0)


CM_ROWS = 64


def _cm_body(v_ref, g_ref, w_ref, b_ref, lg_ref, lb_ref, p_ref, o_ref, s_ref, buf, ybuf, *, t):
    @pl.when(pl.program_id(1) == 0)
    def _():
        buf[0:CM_HIST, :] = p_ref[0]

    buf[CM_HIST:CM_HIST + t, :] = v_ref[...] * _sigmoid(g_ref[...])
    base = CM_HIST - (CM_K - 1)
    rt = min(CM_ROWS, t)
    win_rows = rt + CM_HIST

    def cols(ci, carry):
        cs = pl.ds(pl.multiple_of(ci * LANES, LANES), LANES)
        for r0 in range(0, t, rt):
            win = buf[r0:r0 + win_rows, cs]
            acc = b_ref[:, cs] + w_ref[CM_K - 1:CM_K, cs] * buf[r0 + CM_HIST:r0 + CM_HIST + rt, cs]
            for s in range(SUBLANES):
                sh = win if s == 0 else pltpu.roll(win, win_rows - s, 0)
                for a in range(CM_HIST // SUBLANES):
                    k = SUBLANES * a + s - base
                    if 0 <= k < CM_K - 1:
                        acc = acc + w_ref[k:k + 1, cs] * sh[SUBLANES * a:SUBLANES * a + rt, :]
            ybuf[r0:r0 + rt, cs] = acc
        return carry

    lax.fori_loop(0, D_MODEL // LANES, cols, 0)

    hist = buf[t:t + CM_HIST, :]
    buf[0:CM_HIST, :] = hist

    @pl.when(pl.program_id(1) == pl.num_programs(1) - 1)
    def _():
        s_ref[0] = hist

    y = ybuf[...]
    mu = jnp.mean(y, axis=-1, keepdims=True)
    yc = y - mu
    var = jnp.mean(yc * yc, axis=-1, keepdims=True)
    o_ref[...] = _silu(yc * lax.rsqrt(var + EPS) * lg_ref[...] + lb_ref[...]).astype(o_ref.dtype)


def _convmod(proj, row0, nb, seq, dw_w, dw_b, ln_g, ln_b, prev):
    t = min(256, seq)
    nc = seq // t
    rb = row0 // t
    vi = 0
    rows = lambda b, c: rb + b * nc + c
    full = lambda shape: pl.BlockSpec(shape, lambda b, c: (0, 0))
    return pl.pallas_call(
        functools.partial(_cm_body, t=t),
        grid=(nb, nc),
        in_specs=[
            pl.BlockSpec((t, D_MODEL), lambda b, c: (rows(b, c), vi)),
            pl.BlockSpec((t, D_MODEL), lambda b, c: (rows(b, c), vi + 1)),
            full((CM_K, D_MODEL)), full((1, D_MODEL)), full((1, D_MODEL)), full((1, D_MODEL)),
            pl.BlockSpec((1, CM_HIST, D_MODEL), lambda b, c: (b, 0, 0)),
        ],
        out_specs=[
            pl.BlockSpec((t, D_MODEL), lambda b, c: (b * nc + c, 0)),
            pl.BlockSpec((1, CM_HIST, D_MODEL), lambda b, c: (b, 0, 0)),
        ],
        out_shape=[
            jax.ShapeDtypeStruct((nb * seq, D_MODEL), BF16),
            jax.ShapeDtypeStruct((nb, CM_HIST, D_MODEL), F32),
        ],
        scratch_shapes=[pltpu.VMEM((CM_HIST + t, D_MODEL), F32), pltpu.VMEM((t, D_MODEL), F32)],
        compiler_params=_params(("parallel", "arbitrary")),
        name="convmod",
    )(proj, proj, dw_w, dw_b, ln_g, ln_b, prev)


def _attn_body(q_ref, k_ref, v_ref, o_ref, kb, vb, *, per_head):
    @pl.when(pl.program_id(1) == 0)
    def _():
        if per_head:
            for h in range(XA_HEADS):
                hs = slice(h * XA_DIM, (h + 1) * XA_DIM)
                kb[:, hs] = k_ref[0, :, h, :].astype(BF16)
                vb[:, hs] = v_ref[0, :, h, :].astype(BF16)
        else:
            kb[...] = k_ref[0].astype(BF16)
            vb[...] = v_ref[0].astype(BF16)

    nt = (((1,), (1,)), ((), ()))
    for h in range(XA_HEADS):
        hs = slice(h * XA_DIM, (h + 1) * XA_DIM)
        s = lax.dot_general(q_ref[:, hs], kb[:, hs], nt, preferred_element_type=F32) * (XA_DIM ** -0.5)
        e = jnp.exp(s - jnp.max(s, axis=-1, keepdims=True))
        p = (e / jnp.sum(e, axis=-1, keepdims=True)).astype(BF16)
        o_ref[:, hs] = jnp.dot(p, vb[:, hs], preferred_element_type=F32).astype(o_ref.dtype)


def _attend(q, row0, nb, seq, k, v):
    tq = min(512, seq)
    nq = seq // tq
    rb = row0 // tq
    per_head = k.ndim == 4
    if per_head:
        kv_spec = pl.BlockSpec((1, N_MEM, XA_HEADS, XA_DIM), lambda b, i: (b, 0, 0, 0))
    else:
        kv_spec = pl.BlockSpec((1, N_MEM, D_MODEL), lambda b, i: (b, 0, 0))
    return pl.pallas_call(
        functools.partial(_attn_body, per_head=per_head),
        grid=(nb, nq),
        in_specs=[pl.BlockSpec((tq, D_MODEL), lambda b, i: (rb + b * nq + i, 0)), kv_spec, kv_spec],
        out_specs=pl.BlockSpec((tq, D_MODEL), lambda b, i: (b * nq + i, 0)),
        out_shape=jax.ShapeDtypeStruct((nb * seq, D_MODEL), BF16),
        scratch_shapes=[pltpu.VMEM((N_MEM, D_MODEL), BF16), pltpu.VMEM((N_MEM, D_MODEL), BF16)],
        compiler_params=_params(("parallel", "arbitrary")),
        name="attend",
    )(q, k, v)


def _pad_rows_front(a, rows):
    return jnp.pad(a, ((0, 0), (rows - a.shape[1], 0), (0, 0)))


def kernel(x_prompt, x_sample, mem_prompt, cache_ssd_conv, cache_ssm_state, cache_cm_conv, cache_mem_k, cache_mem_v, ffn1_norm, ffn1_wi, ffn1_wo, mix_norm, w_in, ssd_conv_w, ssd_conv_b, ssd_dt_bias, ssd_a_log, ssd_d, ssd_norm, ssd_w_out, cm_dw_w, cm_dw_b, cm_ln_g, cm_ln_b, cm_w_out, w_mix_out, xa_norm, mem_norm, xa_wq, xa_wk, xa_wv, xa_wo, ffn2_norm, ffn2_wi, ffn2_wo, final_norm):
    assert ffn1_wi.shape[0] == 1, "single layer"
    bp, lp, _ = x_prompt.shape
    bs, ls, _ = x_sample.shape
    np_rows, ns_rows = bp * lp, bs * ls

    w_t = jnp.transpose(w_in[0])
    w_t_b = w_t.astype(BF16)
    w_dt_t = jnp.pad(w_t[SSD_COLS:SSD_COLS + N_HEADS], ((0, LANES - N_HEADS), (0, 0)))
    w_cm_t = w_t_b[SSD_COLS + N_HEADS:]
    cast = lambda a: a[0].astype(BF16)

    x1 = _ffn(x_prompt.reshape(np_rows, D_MODEL), x_sample.reshape(ns_rows, D_MODEL), ffn1_norm[0],
              ffn1_wi[0], ffn1_wo[0])
    proj = _norm_mm(x1, mix_norm[0], w_t_b, bm=1040, bn=1024, w_t=True, n=SSD_COLS, name="in_proj_ssd")
    proj_cm = _norm_mm(x1, mix_norm[0], w_cm_t, bm=1040, bn=1024, w_t=True, name="in_proj_cm")
    pdt = _norm_mm(x1, mix_norm[0], w_dt_t, bm=1040, bn=LANES, w_t=True, name="dt_proj")

    pad_h = lambda a: jnp.pad(a[0].reshape(1, N_HEADS), ((0, 0), (0, LANES - N_HEADS)))
    d_exp = jnp.repeat(ssd_d[0], HEADDIM).reshape(1, D_INNER)
    expand = (lax.broadcasted_iota(jnp.int32, (LANES, D_INNER), 1) // HEADDIM
              == lax.broadcasted_iota(jnp.int32, (LANES, D_INNER), 0)).astype(BF16)
    ssd_args = (ssd_conv_w[0], ssd_conv_b[0].reshape(1, XBC), pad_h(ssd_dt_bias), pad_h(ssd_a_log), d_exp,
                ssd_norm[0].reshape(1, D_INNER), jnp.concatenate([expand] * 3, axis=0))
    yn_p, hp = _ssd(proj, pdt, 0, bp, lp, *ssd_args,
                    jnp.zeros((bp, SSD_HIST, XBC), F32), jnp.zeros((bp, D_INNER, D_STATE), F32))
    yn_s, hs = _ssd(proj, pdt, np_rows, bs, ls, *ssd_args,
                    _pad_rows_front(cache_ssd_conv[0], SSD_HIST), cache_ssm_state[0].reshape(bs, D_INNER, D_STATE))
    xbc_cols = slice(D_INNER, D_INNER + XBC)
    p_ssd_conv = jnp.stack([proj[(b + 1) * lp - (SSD_K - 1):(b + 1) * lp, xbc_cols] for b in range(bp)])
    s_ssd_conv = jnp.concatenate(
        [cache_ssd_conv[0], proj[np_rows:, xbc_cols].reshape(bs, ls, XBC)], axis=1)[:, -(SSD_K - 1):]

    cm_args = (cm_dw_w[0], cm_dw_b[0].reshape(1, D_MODEL), cm_ln_g[0].reshape(1, D_MODEL),
               cm_ln_b[0].reshape(1, D_MODEL))
    un_p, cp = _convmod(proj_cm, 0, bp, lp, *cm_args, jnp.zeros((bp, CM_HIST, D_MODEL), F32))
    un_s, cs = _convmod(proj_cm, np_rows, bs, ls, *cm_args, _pad_rows_front(cache_cm_conv[0], CM_HIST))

    m = _merge(yn_p, yn_s, un_p, un_s, cast(ssd_w_out), cast(cm_w_out), proj_cm)
    x2 = _mm_res(m, None, w_mix_out[0], x1, bm=1040, bn=1024, name="mix_out")

    mem = mem_prompt.reshape(bp * N_MEM, D_MODEL)
    k_p = _norm_mm(mem, mem_norm[0], xa_wk[0], bm=bp * N_MEM, bn=1024, name="mem_k")
    v_p = _norm_mm(mem, mem_norm[0], xa_wv[0], bm=bp * N_MEM, bn=1024, name="mem_v")
    q = _norm_mm(x2, xa_norm[0], xa_wq[0], bm=1040, bn=1024, out_dtype=BF16, name="q_proj")
    o_p = _attend(q, 0, bp, lp, k_p.reshape(bp, N_MEM, D_MODEL), v_p.reshape(bp, N_MEM, D_MODEL))
    o_s = _attend(q, np_rows, bs, ls, cache_mem_k[0], cache_mem_v[0])
    x3 = _mm_res(o_p, o_s, xa_wo[0], x2, bm=1040, bn=1024, name="attn_out")

    y_p, y_s = _ffn(x3, None, ffn2_norm[0], ffn2_wi[0], ffn2_wo[0], final_g=final_norm, split_rows=np_rows)

    kv_shape = (1, bp, N_MEM, XA_HEADS, XA_DIM)
    state_shape = lambda b: (1, b, N_HEADS, HEADDIM, D_STATE)
    return (
        y_p.reshape(bp, lp, D_MODEL),
        y_s.reshape(bs, ls, D_MODEL),
        p_ssd_conv[None],
        hp.reshape(state_shape(bp)),
        cp[:, CM_HIST - (CM_K - 1):][None],
        k_p.reshape(kv_shape),
        v_p.reshape(kv_shape),
        s_ssd_conv[None],
        hs.reshape(state_shape(bs)),
        cs[:, CM_HIST - (CM_K - 1):][None],
    )
```
